```python
import math
import jax, jax.numpy as jnp
from jax import lax
import numpy as np

D_MODEL = 4096
BATCH = 4
SEQ = 2048
DEPTH = 1
DEC_BATCH = 128
DEC_SEQ = 8
PAST_LEN = 16384
PAGE_SIZE = 128

SSM_WIDTH = D_MODEL // 2
SSM_GROUP = 16
SSM_GROUPS = SSM_WIDTH // SSM_GROUP
SSM_STATE = 64
DT_MIN = 1e-3
DT_MAX = 1e-1
CONV_WIDTH = D_MODEL // 2
CONV_K = 31
PK_HEADS = 8
PK_NKEYS = 128
PK_EXPERTS = PK_NKEYS * PK_NKEYS
PK_QDIM = 256
PK_HALF = PK_QDIM // 2
PK_TOPK = 16
PK_BLOCK = 128
IN_COLS = SSM_WIDTH + 2 * CONV_WIDTH + 2 * D_MODEL
N_MOD = 6
EPS = 1e-6

kernel_name = 'gated_s5_conformer_peer_adaln_step'


def _rmsnorm(x, g):
    xf = x.astype(jnp.float32)
    y = xf * lax.rsqrt(jnp.mean(xf * xf, axis=-1, keepdims=True) + EPS)
    return (y * g.astype(jnp.float32)).astype(x.dtype)


def _layernorm(x, g, b):
    xf = x.astype(jnp.float32)
    mu = jnp.mean(xf, axis=-1, keepdims=True)
    var = jnp.mean(jnp.square(xf - mu), axis=-1, keepdims=True)
    y = (xf - mu) * lax.rsqrt(var + EPS)
    return (y * g.astype(jnp.float32) + b.astype(jnp.float32)).astype(x.dtype)


def _cmul(ar, ai, br, bi):
    return ar * br - ai * bi, ar * bi + ai * br


def _ssm_branch(u, h0_re, h0_im, lam_re, lam_im, log_dt, b_re, b_im, c_re, c_im, d_skip):
    f32 = jnp.float32
    bsz, T, _ = u.shape
    lr = lam_re.astype(f32)
    li = lam_im.astype(f32)
    dt = jnp.exp(log_dt.astype(f32))[:, None]
    decay = jnp.exp(lr * dt)
    ab_re, ab_im = decay * jnp.cos(li * dt), decay * jnp.sin(li * dt)
    den = lr * lr + li * li
    f_re = ((ab_re - 1.0) * lr + ab_im * li) / den
    f_im = (ab_im * lr - (ab_re - 1.0) * li) / den
    bb_re, bb_im = _cmul(f_re[..., None], f_im[..., None], b_re.astype(f32), b_im.astype(f32))
    ug = u.astype(f32).reshape(bsz, T, SSM_GROUPS, SSM_GROUP)
    bu_re = jnp.einsum('btgp,gnp->btgn', ug, bb_re)
    bu_im = jnp.einsum('btgp,gnp->btgn', ug, bb_im)
    a_re = jnp.broadcast_to(ab_re, bu_re.shape)
    a_im = jnp.broadcast_to(ab_im, bu_im.shape)

    def combine(left, right):
        a1r, a1i, b1r, b1i = left
        a2r, a2i, b2r, b2i = right
        ar, ai = _cmul(a2r, a2i, a1r, a1i)
        br, bi = _cmul(a2r, a2i, b1r, b1i)
        return ar, ai, br + b2r, bi + b2i

    _, _, h_re, h_im = lax.associative_scan(combine, (a_re, a_im, bu_re, bu_im), axis=1)
    if h0_re is not None:
        kdt = jnp.arange(1, T + 1, dtype=f32)[:, None, None] * dt[None]
        pdec = jnp.exp(lr[None] * kdt)
        p_re, p_im = pdec * jnp.cos(li[None] * kdt), pdec * jnp.sin(li[None] * kdt)
        q_re, q_im = _cmul(p_re[None], p_im[None], h0_re.astype(f32)[:, None], h0_im.astype(f32)[:, None])
        h_re = h_re + q_re
        h_im = h_im + q_im
    y = (jnp.einsum('btgn,gpn->btgp', h_re, c_re.astype(f32))
         - jnp.einsum('btgn,gpn->btgp', h_im, c_im.astype(f32))
         + d_skip.astype(f32).reshape(SSM_GROUPS, SSM_GROUP) * ug)
    return (y.reshape(bsz, T, SSM_WIDTH).astype(u.dtype),
            h_re[:, -1].astype(u.dtype), h_im[:, -1].astype(u.dtype))


def _conv_branch(v, buf, w_dw, b_dw, ln_g, ln_b, w_pw):
    a, g = jnp.split(v, 2, axis=-1)
    u = a * jax.nn.sigmoid(g)
    if buf is None:
        buf = jnp.zeros((u.shape[0], CONV_K - 1, CONV_WIDTH), u.dtype)
    up = jnp.concatenate([buf.astype(u.dtype), u], axis=1)
    out = lax.conv_general_dilated(up, w_dw.astype(u.dtype)[:, None, :], (1,), 'VALID',
                                   dimension_numbers=('NWC', 'WIO', 'NWC'),
                                   feature_group_count=CONV_WIDTH) + b_dw
    out = jax.nn.silu(_layernorm(out, ln_g, ln_b))
    return out @ w_pw, up[:, -(CONV_K - 1):]


def _peer(h, w_q, sub_keys, u_tab, v_tab):
    bsz, T, D = h.shape
    tok = h.reshape(bsz * T, D)
    n = tok.shape[0]
    n_pad = (-n) % PK_BLOCK
    tok = jnp.pad(tok, ((0, n_pad), (0, 0)))
    blocks = tok.reshape(-1, PK_BLOCK, D)
    keys = sub_keys.astype(jnp.float32)

    def one_block(xb):
        q = (xb @ w_q).reshape(PK_BLOCK, PK_HEADS, 2, PK_HALF).astype(jnp.float32)
        s = jnp.einsum('thpd,hpkd->thpk', q, keys)
        sv, si = lax.top_k(s, PK_TOPK)
        cand = (sv[:, :, 0, :, None] + sv[:, :, 1, None, :]).reshape(PK_BLOCK, PK_HEADS, PK_TOPK * PK_TOPK)
        cidx = (si[:, :, 0, :, None] * PK_NKEYS + si[:, :, 1, None, :]).reshape(PK_BLOCK, PK_HEADS, PK_TOPK * PK_TOPK)
        best, pos = lax.top_k(cand, PK_TOPK)
        eidx = jnp.take_along_axis(cidx, pos, axis=-1)
        gate = jax.nn.softmax(best, axis=-1)
        u_sel = u_tab[eidx]
        act = jax.nn.gelu(jnp.einsum('td,thkd->thk', xb, u_sel).astype(jnp.float32))
        w = (gate * act).astype(xb.dtype)
        return jnp.einsum('thk,thkd->td', w, v_tab[eidx])

    out = lax.map(one_block, blocks)
    return out.reshape(-1, D)[:n].reshape(bsz, T, D)


def _layer(x, c, h0_re, h0_im, conv_buf,
           w_mod, b_mod, norm_pre_mix, norm_post_mix, norm_pre_ffn, norm_post_ffn,
           w_in, ssm_lambda_re, ssm_lambda_im, ssm_log_dt, ssm_b_re, ssm_b_im, ssm_c_re, ssm_c_im,
           ssm_d, ssm_w_glu, ssm_b_glu, w_proj_ssm,
           conv_w_dw, conv_b_dw, conv_ln_g, conv_ln_b, conv_w_pw, w_out,
           peer_w_q, peer_sub_keys, peer_u, peer_v):
    bsz = x.shape[0]
    mod = (jax.nn.silu(c) @ w_mod + b_mod).reshape(bsz, N_MOD, D_MODEL)
    sh1, sc1, g1, sh2, sc2, g2 = [mod[:, i, None, :] for i in range(N_MOD)]

    h = _rmsnorm(x, norm_pre_mix) * (1.0 + sc1) + sh1
    proj = h @ w_in
    u_ssm = proj[..., :SSM_WIDTH]
    v_conv = proj[..., SSM_WIDTH:SSM_WIDTH + 2 * CONV_WIDTH]
    gate_ssm = proj[..., SSM_WIDTH + 2 * CONV_WIDTH:SSM_WIDTH + 2 * CONV_WIDTH + D_MODEL]
    gate_conv = proj[..., SSM_WIDTH + 2 * CONV_WIDTH + D_MODEL:]

    y_ssm, new_re, new_im = _ssm_branch(u_ssm, h0_re, h0_im, ssm_lambda_re, ssm_lambda_im, ssm_log_dt,
                                        ssm_b_re, ssm_b_im, ssm_c_re, ssm_c_im, ssm_d)
    y_ssm = jax.nn.gelu(y_ssm)
    y_ssm = y_ssm * jax.nn.sigmoid(y_ssm @ ssm_w_glu + ssm_b_glu)
    y_a = y_ssm @ w_proj_ssm
    y_b, new_buf = _conv_branch(v_conv, conv_buf, conv_w_dw, conv_b_dw, conv_ln_g, conv_ln_b, conv_w_pw)
    mix = (jax.nn.sigmoid(gate_ssm) * y_a + jax.nn.sigmoid(gate_conv) * y_b) @ w_out
    x = x + g1 * _rmsnorm(mix, norm_post_mix)

    h = _rmsnorm(x, norm_pre_ffn) * (1.0 + sc2) + sh2
    ff = _peer(h, peer_w_q, peer_sub_keys, peer_u, peer_v)
    x = x + g2 * _rmsnorm(ff, norm_post_ffn)
    return x, new_re, new_im, new_buf


def setup_inputs(seed: int = 0) -> dict:
    key = jax.random.key(seed)
    it = iter(list(jax.random.split(key, 48)))
    f32 = jnp.float32
    L = DEPTH

    def nrm(shape, scale):
        return jax.random.normal(next(it), shape, f32) * scale

    n_idx = jnp.arange(SSM_STATE, dtype=f32)
    return {
        'x_prompt': nrm((BATCH, SEQ, D_MODEL), 1.0),
        'x_sample': nrm((DEC_BATCH, DEC_SEQ, D_MODEL), 1.0),
        'c_prompt': nrm((BATCH, D_MODEL), 1.0),
        'c_sample': nrm((DEC_BATCH, D_MODEL), 1.0),
        'state_ssm_re': nrm((L, DEC_BATCH, SSM_GROUPS, SSM_STATE), 0.5),
        'state_ssm_im': nrm((L, DEC_BATCH, SSM_GROUPS, SSM_STATE), 0.5),
        'state_conv': nrm((L, DEC_BATCH, CONV_K - 1, CONV_WIDTH), 0.5),
        'w_mod': nrm((L, D_MODEL, N_MOD * D_MODEL), 0.5 * D_MODEL ** -0.5),
        'b_mod': nrm((L, N_MOD * D_MODEL), 0.01),
        'norm_pre_mix': 1.0 + nrm((L, D_MODEL), 0.02),
        'norm_post_mix': 1.0 + nrm((L, D_MODEL), 0.02),
        'norm_pre_ffn': 1.0 + nrm((L, D_MODEL), 0.02),
        'norm_post_ffn': 1.0 + nrm((L, D_MODEL), 0.02),
        'w_in': nrm((L, D_MODEL, IN_COLS), D_MODEL ** -0.5),
        'ssm_lambda_re': -0.5 + nrm((L, SSM_GROUPS, SSM_STATE), 0.01),
        'ssm_lambda_im': math.pi * n_idx + nrm((L, SSM_GROUPS, SSM_STATE), 0.01),
        'ssm_log_dt': jax.random.uniform(next(it), (L, SSM_GROUPS), f32, math.log(DT_MIN), math.log(DT_MAX)),
        'ssm_b_re': nrm((L, SSM_GROUPS, SSM_STATE, SSM_GROUP), (2 * SSM_GROUP) ** -0.5),
        'ssm_b_im': nrm((L, SSM_GROUPS, SSM_STATE, SSM_GROUP), (2 * SSM_GROUP) ** -0.5),
        'ssm_c_re': nrm((L, SSM_GROUPS, SSM_GROUP, SSM_STATE), (2 * SSM_STATE) ** -0.5),
        'ssm_c_im': nrm((L, SSM_GROUPS, SSM_GROUP, SSM_STATE), (2 * SSM_STATE) ** -0.5),
        'ssm_d': nrm((L, SSM_WIDTH), 1.0),
        'ssm_w_glu': nrm((L, SSM_WIDTH, SSM_WIDTH), SSM_WIDTH ** -0.5),
        'ssm_b_glu': nrm((L, SSM_WIDTH), 0.01),
        'w_proj_ssm': nrm((L, SSM_WIDTH, D_MODEL), SSM_WIDTH ** -0.5),
        'conv_w_dw': nrm((L, CONV_K, CONV_WIDTH), CONV_K ** -0.5),
        'conv_b_dw': nrm((L, CONV_WIDTH), 0.01),
        'conv_ln_g': 1.0 + nrm((L, CONV_WIDTH), 0.02),
        'conv_ln_b': nrm((L, CONV_WIDTH), 0.01),
        'conv_w_pw': nrm((L, CONV_WIDTH, D_MODEL), CONV_WIDTH ** -0.5),
        'w_out': nrm((L, D_MODEL, D_MODEL), D_MODEL ** -0.5),
        'peer_w_q': nrm((L, D_MODEL, PK_HEADS * PK_QDIM), D_MODEL ** -0.5),
        'peer_sub_keys': nrm((L, PK_HEADS, 2, PK_NKEYS, PK_HALF), PK_HALF ** -0.5),
        'peer_u': nrm((L, PK_EXPERTS, D_MODEL), D_MODEL ** -0.5),
        'peer_v': nrm((L, PK_EXPERTS, D_MODEL), 1.0),
    }


def reference(x_prompt, x_sample, c_prompt, c_sample, state_ssm_re, state_ssm_im, state_conv,
              w_mod, b_mod, norm_pre_mix, norm_post_mix, norm_pre_ffn, norm_post_ffn,
              w_in, ssm_lambda_re, ssm_lambda_im, ssm_log_dt, ssm_b_re, ssm_b_im, ssm_c_re, ssm_c_im,
              ssm_d, ssm_w_glu, ssm_b_glu, w_proj_ssm,
              conv_w_dw, conv_b_dw, conv_ln_g, conv_ln_b, conv_w_pw, w_out,
              peer_w_q, peer_sub_keys, peer_u, peer_v):
    xp, xs = x_prompt, x_sample
    p_re, p_im, p_conv, s_re, s_im, s_conv = [], [], [], [], [], []
    for l in range(DEPTH):
        weights = (w_mod[l], b_mod[l], norm_pre_mix[l], norm_post_mix[l], norm_pre_ffn[l], norm_post_ffn[l],
                   w_in[l], ssm_lambda_re[l], ssm_lambda_im[l], ssm_log_dt[l], ssm_b_re[l], ssm_b_im[l],
                   ssm_c_re[l], ssm_c_im[l], ssm_d[l], ssm_w_glu[l], ssm_b_glu[l], w_proj_ssm[l],
                   conv_w_dw[l], conv_b_dw[l], conv_ln_g[l], conv_ln_b[l], conv_w_pw[l], w_out[l],
                   peer_w_q[l], peer_sub_keys[l], peer_u[l], peer_v[l])
        xp, hre, him, cbuf = _layer(xp, c_prompt, None, None, None, *weights)
        xs, sre, sim, sbuf = _layer(xs, c_sample, state_ssm_re[l], state_ssm_im[l], state_conv[l], *weights)
        p_re.append(hre)
        p_im.append(him)
        p_conv.append(cbuf)
        s_re.append(sre)
        s_im.append(sim)
        s_conv.append(sbuf)
    return (xp, xs, jnp.stack(p_re), jnp.stack(p_im), jnp.stack(p_conv),
            jnp.stack(s_re), jnp.stack(s_im), jnp.stack(s_conv))
```

```python
import functools
import math

import jax
import jax.numpy as jnp
from jax import lax
from jax.experimental import pallas as pl
from jax.experimental.pallas import tpu as pltpu

F32 = jnp.float32
BF16 = jnp.bfloat16
HIGHEST = lax.Precision.HIGHEST

EPS = 1e-6
N_MOD = 6
SSM_GROUP = 16
SSM_STATE = 64
CONV_K = 31
CONV_HIST = 32
PK_HEADS = 8
PK_NKEYS = 128
PK_TOPK = 16
SSM_CHUNK = 16

V7X_VMEM_BYTES = 64 * 1024 * 1024
VMEM_CAP_MB = 56
LANES = 128
ROW_TILE = 512
NORM_ROW_TILE = 256
COL_TILE = 1024

PK_CANDS = tuple((k0, k1) for k0 in range(PK_TOPK) for k1 in range(PK_TOPK)
                 if (k0 + 1) * (k1 + 1) <= PK_TOPK)
PK_CAND_ROWS = 56


def _cparams(*sem):
    assert VMEM_CAP_MB * 1024 * 1024 <= V7X_VMEM_BYTES
    return pltpu.CompilerParams(dimension_semantics=sem, vmem_limit_bytes=VMEM_CAP_MB * 1024 * 1024)


def _sigmoid(x):
    return 1.0 / (1.0 + jnp.exp(-x))


def _gelu_tanh(x):
    c = math.sqrt(2.0 / math.pi)
    return 0.5 * x * (1.0 + jnp.tanh(c * (x + 0.044715 * (x * x * x))))


def _rms(x, g):
    return x * lax.rsqrt(jnp.mean(x * x, axis=-1, keepdims=True) + EPS) * g


def _mod_kernel(c_ref, w_ref, b_ref, o_ref):
    c = c_ref[...]
    a = (c * _sigmoid(c)).astype(BF16)
    o_ref[...] = jnp.dot(a, w_ref[...].astype(BF16), preferred_element_type=F32) + b_ref[...]


def _mod_call(c_all, w_mod, b_mod):
    m, d = c_all.shape
    n = w_mod.shape[1]
    tn = 512
    return pl.pallas_call(
        _mod_kernel,
        grid=(n // tn,),
        in_specs=[pl.BlockSpec((m, d), lambda j: (0, 0)),
                  pl.BlockSpec((d, tn), lambda j: (0, j)),
                  pl.BlockSpec((1, tn), lambda j: (0, j))],
        out_specs=pl.BlockSpec((m, tn), lambda j: (0, j)),
        out_shape=jax.ShapeDtypeStruct((m, n), F32),
        compiler_params=_cparams("arbitrary"),
        name="mod",
    )(c_all, w_mod, b_mod.reshape(1, n))


class _Mod:
    def __init__(self, arr, time_major, seq, d):
        self.arr, self.time_major, self.seq, self.d = arr, time_major, seq, d

    def x_spec(self, rows):
        if self.time_major:
            assert rows % self.seq == 0
            return pl.BlockSpec((rows // self.seq, self.seq, self.d), lambda i, *_: (i, 0, 0))
        assert self.seq % rows == 0
        tpb = self.seq // rows
        return pl.BlockSpec((1, rows, self.d), lambda i, *_: (i // tpb, i % tpb, 0))

    def spec(self, k, rows):
        if self.time_major:
            return pl.BlockSpec((self.seq, self.d), lambda i, *_: (0, k))
        tpb = self.seq // rows
        return pl.BlockSpec((None, 1, self.d), lambda i, *_: (i // tpb, 0, k))


NORM_CHUNK_ROWS = 64


def _for_row_chunks(tile_shape, fn):
    n_lead, n_rows, _ = tile_shape
    cr = NORM_CHUNK_ROWS
    per_lead = n_rows // cr

    def body(c, carry):
        sub = pl.ds(pl.multiple_of((c % per_lead) * cr, cr), cr)
        fn(c // per_lead, sub, pl.ds(pl.multiple_of(c * cr, cr), cr))
        return carry

    lax.fori_loop(0, n_lead * per_lead, body, 0)


def _mod_rows(ref, sub):
    return ref[...] if ref.shape[0] == 1 else ref[sub, :]


def _inproj_kernel(x_ref, sh_ref, sc_ref, g_ref, w_ref, o_ref, h_scr):
    @pl.when(pl.program_id(1) == 0)
    def _():
        def chunk(lead, sub, flat):
            h = _rms(x_ref[lead, sub, :], g_ref[...]) * (1.0 + _mod_rows(sc_ref, sub)) + _mod_rows(sh_ref, sub)
            h_scr[flat, :] = h.astype(BF16)

        _for_row_chunks(x_ref.shape, chunk)

    o_ref[...] = jnp.dot(h_scr[...], w_ref[...], preferred_element_type=F32)


def _inproj_call(x3, mod, g, w_b):
    n_rows, d = x3.shape[0] * x3.shape[1], x3.shape[2]
    n = w_b.shape[1]
    tm, tn = ROW_TILE, COL_TILE
    return pl.pallas_call(
        _inproj_kernel,
        grid=(n_rows // tm, n // tn),
        in_specs=[mod.x_spec(tm), mod.spec(0, tm), mod.spec(1, tm),
                  pl.BlockSpec((1, d), lambda i, j: (0, 0)),
                  pl.BlockSpec((d, tn), lambda i, j: (0, j))],
        out_specs=pl.BlockSpec((tm, tn), lambda i, j: (i, j)),
        out_shape=jax.ShapeDtypeStruct((n_rows, n), F32),
        scratch_shapes=[pltpu.VMEM((tm, d), BF16)],
        compiler_params=_cparams("arbitrary", "arbitrary"),
        name="inproj",
    )(x3, mod.arr, mod.arr, g.reshape(1, d), w_b)


def _glu_kernel(y_ref, yt_ref, w_ref, b_ref, o_ref, a_scr):
    @pl.when(pl.program_id(1) == 0)
    def _():
        a_scr[...] = _gelu_tanh(y_ref[...]).astype(BF16)

    acc = jnp.dot(a_scr[...], w_ref[...], preferred_element_type=F32) + b_ref[...]
    o_ref[...] = (_gelu_tanh(yt_ref[...]) * _sigmoid(acc)).astype(BF16)


def _glu_call(y2d, w_b, b):
    n_rows, k = y2d.shape
    n = w_b.shape[1]
    tm, tn = ROW_TILE, COL_TILE
    return pl.pallas_call(
        _glu_kernel,
        grid=(n_rows // tm, n // tn),
        in_specs=[pl.BlockSpec((tm, k), lambda i, j: (i, 0)),
                  pl.BlockSpec((tm, tn), lambda i, j: (i, j)),
                  pl.BlockSpec((k, tn), lambda i, j: (0, j)),
                  pl.BlockSpec((1, tn), lambda i, j: (0, j))],
        out_specs=pl.BlockSpec((tm, tn), lambda i, j: (i, j)),
        out_shape=jax.ShapeDtypeStruct((n_rows, n), BF16),
        scratch_shapes=[pltpu.VMEM((tm, k), BF16)],
        compiler_params=_cparams("arbitrary", "arbitrary"),
        name="ssm_glu",
    )(y2d, y2d, w_b, b.reshape(1, n))


def _gated_proj_kernel(z_ref, w_ref, gate_ref, o_ref):
    acc = jnp.dot(z_ref[...].astype(BF16), w_ref[...], preferred_element_type=F32)
    o_ref[...] = _sigmoid(gate_ref[...]) * acc


def _gated_add_proj_kernel(z_ref, w_ref, gate_ref, a_ref, o_ref):
    acc = jnp.dot(z_ref[...].astype(BF16), w_ref[...], preferred_element_type=F32)
    o_ref[...] = (a_ref[...] + _sigmoid(gate_ref[...]) * acc).astype(o_ref.dtype)


def _gated_proj_call(z2d, w_b, proj, gate_col0, add=None, out_dtype=F32):
    n_rows, k = z2d.shape
    n = w_b.shape[1]
    tm, tn = ROW_TILE, COL_TILE
    gj = gate_col0 // tn
    in_specs = [pl.BlockSpec((tm, k), lambda i, j: (i, 0)),
                pl.BlockSpec((k, tn), lambda i, j: (0, j)),
                pl.BlockSpec((tm, tn), lambda i, j: (i, gj + j))]
    args = [z2d, w_b, proj]
    if add is not None:
        in_specs.append(pl.BlockSpec((tm, tn), lambda i, j: (i, j)))
        args.append(add)
    return pl.pallas_call(
        _gated_proj_kernel if add is None else _gated_add_proj_kernel,
        grid=(n_rows // tm, n // tn),
        in_specs=in_specs,
        out_specs=pl.BlockSpec((tm, tn), lambda i, j: (i, j)),
        out_shape=jax.ShapeDtypeStruct((n_rows, n), out_dtype),
        compiler_params=_cparams("arbitrary", "arbitrary"),
        name="gated_proj" if add is None else "gated_add_proj",
    )(*args)


def _matmul_kernel(a_ref, w_ref, o_ref):
    o_ref[...] = jnp.dot(a_ref[...], w_ref[...], preferred_element_type=F32)


def _matmul_call(a2d, w_b):
    n_rows, k = a2d.shape
    n = w_b.shape[1]
    tm, tn = ROW_TILE, COL_TILE
    return pl.pallas_call(
        _matmul_kernel,
        grid=(n_rows // tm, n // tn),
        in_specs=[pl.BlockSpec((tm, k), lambda i, j: (i, 0)),
                  pl.BlockSpec((k, tn), lambda i, j: (0, j))],
        out_specs=pl.BlockSpec((tm, tn), lambda i, j: (i, j)),
        out_shape=jax.ShapeDtypeStruct((n_rows, n), F32),
        compiler_params=_cparams("arbitrary", "arbitrary"),
        name="out_proj",
    )(a2d, w_b)


def _ssm_param_kernel(lrc_ref, lic_ref, lrr_ref, lir_ref, ldt_ref, ctre_ref, ctim_ref, btre_ref, btim_ref,
                      dpad_ref, t_ref, sre_ref, sim_ref, rre_ref, rim_ref, lre_ref, lim_ref, *, chunk):
    p, n = SSM_GROUP, SSM_STATE
    lp = chunk * p
    sre_ref[...] = jnp.zeros_like(sre_ref)
    sim_ref[...] = jnp.zeros_like(sim_ref)
    rre_ref[...] = jnp.zeros_like(rre_ref)
    rim_ref[...] = jnp.zeros_like(rim_ref)
    lane = lax.broadcasted_iota(jnp.int32, (p, lp), 1)
    diag = lax.broadcasted_iota(jnp.int32, (p, lp), 0) == lane
    k_lane = (lax.broadcasted_iota(jnp.int32, (n, lp), 1) // p).astype(F32)
    e_row = (chunk - 1 - lax.broadcasted_iota(jnp.int32, (chunk, n), 0)).astype(F32)
    for a in range(2):
        dt = jnp.exp(ldt_ref[a])
        lrc, lic = lrc_ref[a] * dt, lic_ref[a] * dt
        lrr, lir = lrr_ref[a], lir_ref[a]
        m0, m1 = jnp.exp(lrc * k_lane), jnp.exp(lrc * (k_lane + 1.0))
        p0re, p0im = m0 * jnp.cos(lic * k_lane), m0 * jnp.sin(lic * k_lane)
        p1re, p1im = m1 * jnp.cos(lic * (k_lane + 1.0)), m1 * jnp.sin(lic * (k_lane + 1.0))
        ctre, ctim = ctre_ref[a], ctim_ref[a]
        dec = jnp.exp(lrr * dt)
        abre, abim = dec * jnp.cos(lir * dt), dec * jnp.sin(lir * dt)
        den = lrr * lrr + lir * lir
        fre = ((abre - 1.0) * lrr + abim * lir) / den
        fim = (abim * lrr - (abre - 1.0) * lir) / den
        bre = fre * btre_ref[a] - fim * btim_ref[a]
        bim = fre * btim_ref[a] + fim * btre_ref[a]
        clre = ctre * p0re - ctim * p0im
        clim = ctre * p0im + ctim * p0re
        kt = (jnp.dot(bre, clre, precision=HIGHEST, preferred_element_type=F32)
              - jnp.dot(bim, clim, precision=HIGHEST, preferred_element_type=F32))
        kt = kt + jnp.where(diag, dpad_ref[a], 0.0)
        for s in range(chunk):
            blk = kt if s == 0 else jnp.where(lane >= s * p, pltpu.roll(kt, s * p, 1), 0.0)
            t_ref[a, s * p:(s + 1) * p, :] = blk.astype(BF16)
        mr = jnp.exp(lrr * dt * e_row)
        prre, prim = mr * jnp.cos(lir * dt * e_row), mr * jnp.sin(lir * dt * e_row)
        for s in range(chunk):
            pr, pi = prre[s:s + 1, :], prim[s:s + 1, :]
            r0 = a * lp + s * p
            sre_ref[0, r0:r0 + p, a * n:(a + 1) * n] = bre * pr - bim * pi
            sim_ref[0, r0:r0 + p, a * n:(a + 1) * n] = bre * pi + bim * pr
        rre_ref[0, a * n:(a + 1) * n, a * lp:(a + 1) * lp] = (ctre * p1re - ctim * p1im).astype(BF16)
        rim_ref[0, a * n:(a + 1) * n, a * lp:(a + 1) * lp] = (-(ctre * p1im + ctim * p1re)).astype(BF16)
        ml = jnp.exp(lrr * dt * float(chunk))
        lre_ref[0, :, a * n:(a + 1) * n] = ml * jnp.cos(lir * dt * float(chunk))
        lim_ref[0, :, a * n:(a + 1) * n] = ml * jnp.sin(lir * dt * float(chunk))


def _ssm_param_call(lam_re, lam_im, log_dt, b_re, b_im, c_re, c_im, d_skip, chunk):
    g, n = lam_re.shape
    p = SSM_GROUP
    lp = chunk * p
    ct_re = jnp.tile(jnp.transpose(c_re, (0, 2, 1)), (1, 1, chunk))
    ct_im = jnp.tile(jnp.transpose(c_im, (0, 2, 1)), (1, 1, chunk))
    bt_re = jnp.transpose(b_re, (0, 2, 1))
    bt_im = jnp.transpose(b_im, (0, 2, 1))
    dpad = jnp.pad(d_skip.reshape(g, 1, p), ((0, 0), (0, 0), (0, lp - p)))
    spec3 = lambda s: pl.BlockSpec((2,) + s, lambda i: (i, 0, 0))
    pair = lambda s: pl.BlockSpec((1,) + s, lambda i: (i, 0, 0))
    return pl.pallas_call(
        functools.partial(_ssm_param_kernel, chunk=chunk),
        grid=(g // 2,),
        in_specs=[spec3((n, 1)), spec3((n, 1)), spec3((1, n)), spec3((1, n)), spec3((1, 1)),
                  spec3((n, lp)), spec3((n, lp)), spec3((p, n)), spec3((p, n)), spec3((1, lp))],
        out_specs=[spec3((lp, lp)), pair((2 * lp, 2 * n)), pair((2 * lp, 2 * n)),
                   pair((2 * n, 2 * lp)), pair((2 * n, 2 * lp)), pair((1, 2 * n)), pair((1, 2 * n))],
        out_shape=[jax.ShapeDtypeStruct((g, lp, lp), BF16),
                   jax.ShapeDtypeStruct((g // 2, 2 * lp, 2 * n), F32),
                   jax.ShapeDtypeStruct((g // 2, 2 * lp, 2 * n), F32),
                   jax.ShapeDtypeStruct((g // 2, 2 * n, 2 * lp), BF16),
                   jax.ShapeDtypeStruct((g // 2, 2 * n, 2 * lp), BF16),
                   jax.ShapeDtypeStruct((g // 2, 1, 2 * n), F32),
                   jax.ShapeDtypeStruct((g // 2, 1, 2 * n), F32)],
        compiler_params=_cparams("arbitrary"),
        name="ssm_params",
    )(lam_re.reshape(g, n, 1), lam_im.reshape(g, n, 1), lam_re.reshape(g, 1, n), lam_im.reshape(g, 1, n),
      log_dt.reshape(g, 1, 1), ct_re, ct_im, bt_re, bt_im, dpad)


def _ssm_main_kernel(u_ref, t_ref, sre_ref, sim_ref, rre_ref, rim_ref, lre_ref, lim_ref, h0re_ref, h0im_ref,
                     y_ref, hnre_ref, hnim_ref, vre_scr, vim_scr, hre_scr, him_scr, *, batch, chunks):
    lp = u_ref.shape[2]
    u0, u1 = u_ref[0], u_ref[1]
    ucat = jnp.concatenate([u0, u1], axis=1)
    vre_scr[...] = jnp.dot(ucat, sre_ref[0], precision=HIGHEST, preferred_element_type=F32)
    vim_scr[...] = jnp.dot(ucat, sim_ref[0], precision=HIGHEST, preferred_element_type=F32)
    are, aim = lre_ref[0], lim_ref[0]
    if chunks == 1:
        hre, him = h0re_ref[0], h0im_ref[0]
        hre_scr[...] = hre
        him_scr[...] = him
        hnre_ref[0] = are * hre - aim * him + vre_scr[...]
        hnim_ref[0] = are * him + aim * hre + vim_scr[...]
    else:
        init = (tuple(h0re_ref[0, b:b + 1, :] for b in range(batch))
                + tuple(h0im_ref[0, b:b + 1, :] for b in range(batch)))

        def body(m, carry):
            nre, nim = [], []
            for b in range(batch):
                hr, hi = carry[b], carry[batch + b]
                row = b * chunks + m
                hre_scr[pl.ds(row, 1), :] = hr
                him_scr[pl.ds(row, 1), :] = hi
                nre.append(are * hr - aim * hi + vre_scr[pl.ds(row, 1), :])
                nim.append(are * hi + aim * hr + vim_scr[pl.ds(row, 1), :])
            return tuple(nre) + tuple(nim)

        fin = lax.fori_loop(0, chunks, body, init)
        for b in range(batch):
            hnre_ref[0, b:b + 1, :] = fin[b]
            hnim_ref[0, b:b + 1, :] = fin[batch + b]
    yh = (jnp.dot(hre_scr[...].astype(BF16), rre_ref[0], preferred_element_type=F32)
          + jnp.dot(him_scr[...].astype(BF16), rim_ref[0], preferred_element_type=F32))
    y_ref[0] = jnp.dot(u0.astype(BF16), t_ref[0], preferred_element_type=F32) + yh[:, :lp]
    y_ref[1] = jnp.dot(u1.astype(BF16), t_ref[1], preferred_element_type=F32) + yh[:, lp:]


def _ssm_main_call(ug, ops, h0re, h0im, batch, chunks):
    t_op, s_re, s_im, r_re, r_im, l_re, l_im = ops
    g, rows, lp = ug.shape
    n2 = 2 * SSM_STATE
    grp = lambda s: pl.BlockSpec((2,) + s, lambda i: (i, 0, 0))
    pair = lambda s: pl.BlockSpec((1,) + s, lambda i: (i, 0, 0))
    return pl.pallas_call(
        functools.partial(_ssm_main_kernel, batch=batch, chunks=chunks),
        grid=(g // 2,),
        in_specs=[grp((rows, lp)), grp((lp, lp)), pair((2 * lp, n2)), pair((2 * lp, n2)),
                  pair((n2, 2 * lp)), pair((n2, 2 * lp)), pair((1, n2)), pair((1, n2)),
                  pair((batch, n2)), pair((batch, n2))],
        out_specs=[grp((rows, lp)), pair((batch, n2)), pair((batch, n2))],
        out_shape=[jax.ShapeDtypeStruct((g, rows, lp), F32),
                   jax.ShapeDtypeStruct((g // 2, batch, n2), F32),
                   jax.ShapeDtypeStruct((g // 2, batch, n2), F32)],
        scratch_shapes=[pltpu.VMEM((rows, n2), F32)] * 4,
        compiler_params=_cparams("arbitrary"),
        name="ssm_main",
    )(ug, t_op, s_re, s_im, r_re, r_im, l_re, l_im, h0re, h0im)


def _ssm_branch(proj3, time_major, h0_re, h0_im, ssm_w):
    lam_re, lam_im, log_dt, b_re, b_im, c_re, c_im, d_skip = ssm_w
    g, n = lam_re.shape
    p = SSM_GROUP
    if time_major:
        t_len, bsz, _ = proj3.shape
    else:
        bsz, t_len, _ = proj3.shape
    chunk = min(SSM_CHUNK, t_len)
    assert t_len % chunk == 0
    chunks = t_len // chunk
    assert not time_major or chunks == 1
    ops = _ssm_param_call(lam_re, lam_im, log_dt, b_re, b_im, c_re, c_im, d_skip, chunk)
    if time_major:
        ug = jnp.transpose(proj3[:, :, :g * p].reshape(t_len, bsz, g, p), (2, 1, 0, 3))
    else:
        ug = jnp.transpose(proj3[:, :, :g * p].reshape(bsz, chunks, chunk, g, p), (3, 0, 1, 2, 4))
    ug = ug.reshape(g, bsz * chunks, chunk * p)
    if h0_re is None:
        h0re = jnp.zeros((g // 2, bsz, 2 * n), F32)
        h0im = h0re
    else:
        h0re = jnp.transpose(h0_re.reshape(bsz, g // 2, 2 * n), (1, 0, 2))
        h0im = jnp.transpose(h0_im.reshape(bsz, g // 2, 2 * n), (1, 0, 2))
    yg, hn_re, hn_im = _ssm_main_call(ug, ops, h0re, h0im, bsz, chunks)
    if time_major:
        y = jnp.transpose(yg.reshape(g, bsz, t_len, p), (2, 1, 0, 3))
    else:
        y = jnp.transpose(yg.reshape(g, bsz, chunks, chunk, p), (1, 2, 3, 0, 4))
    y = y.reshape(bsz * t_len, g * p)
    new_re = jnp.transpose(hn_re, (1, 0, 2)).reshape(bsz, g, n)
    new_im = jnp.transpose(hn_im, (1, 0, 2)).reshape(bsz, g, n)
    return y, new_re, new_im


def _conv_kernel(a_ref, g_ref, buf_ref, w_ref, bdw_ref, lng_ref, lnb_ref, cv_ref, nb_ref, up_scr, acc_scr,
                 *, row_blk, col_blk):
    bb, tt, c = a_ref.shape
    hist = CONV_HIST
    off = hist - (CONV_K - 1)
    j = pl.program_id(1)

    @pl.when(j == 0)
    def _():
        up_scr[:, off:hist, :] = buf_ref[...]

    up_scr[:, hist:hist + tt, :] = a_ref[...] * _sigmoid(g_ref[...])
    for r0 in range(0, tt, row_blk):
        for c0 in range(0, c, col_blk):
            acc = jnp.zeros((bb, row_blk, col_blk), F32) + bdw_ref[:, c0:c0 + col_blk][None]
            for k in range(CONV_K):
                tap = up_scr[:, r0 + k + off:r0 + k + off + row_blk, c0:c0 + col_blk]
                acc = acc + tap * w_ref[k:k + 1, c0:c0 + col_blk][None]
            acc_scr[:, r0:r0 + row_blk, c0:c0 + col_blk] = acc
    v = acc_scr[...]
    mu = jnp.mean(v, axis=-1, keepdims=True)
    vc = v - mu
    var = jnp.mean(vc * vc, axis=-1, keepdims=True)
    y = vc * lax.rsqrt(var + EPS) * lng_ref[...][None] + lnb_ref[...][None]
    cv_ref[...] = y * _sigmoid(y)
    tail = up_scr[:, tt + off:tt + hist, :]

    @pl.when(j == pl.num_programs(1) - 1)
    def _():
        nb_ref[...] = tail

    up_scr[:, off:hist, :] = tail


def _conv_call(proj3, buf, w_dw, b_dw, ln_g, ln_b, c, col0, bb, tt):
    bsz, t_len, _ = proj3.shape
    ja, jg = col0 // c, col0 // c + 1
    row_blk = min(tt, 64)
    vec = lambda: pl.BlockSpec((1, c), lambda i, j: (0, 0))
    return pl.pallas_call(
        functools.partial(_conv_kernel, row_blk=row_blk, col_blk=256),
        grid=(bsz // bb, t_len // tt),
        in_specs=[pl.BlockSpec((bb, tt, c), lambda i, j: (i, j, ja)),
                  pl.BlockSpec((bb, tt, c), lambda i, j: (i, j, jg)),
                  pl.BlockSpec((bb, CONV_K - 1, c), lambda i, j: (i, 0, 0)),
                  pl.BlockSpec((CONV_K, c), lambda i, j: (0, 0)),
                  vec(), vec(), vec()],
        out_specs=[pl.BlockSpec((bb, tt, c), lambda i, j: (i, j, 0)),
                   pl.BlockSpec((bb, CONV_K - 1, c), lambda i, j: (i, 0, 0))],
        out_shape=[jax.ShapeDtypeStruct((bsz, t_len, c), F32),
                   jax.ShapeDtypeStruct((bsz, CONV_K - 1, c), F32)],
        scratch_shapes=[pltpu.VMEM((bb, CONV_HIST + tt, c), F32), pltpu.VMEM((bb, tt, c), F32)],
        compiler_params=_cparams("arbitrary", "arbitrary"),
        name="conv",
    )(proj3, proj3, buf, w_dw, b_dw.reshape(1, c), ln_g.reshape(1, c), ln_b.reshape(1, c))


def _conv_tm_kernel(a_ref, g_ref, buf_ref, w_ref, bdw_ref, lng_ref, lnb_ref, cv_ref, nb_ref, up_scr, *, col_blk):
    t_len, bb, c = a_ref.shape
    nh = CONV_K - 1
    up_scr[0:nh] = buf_ref[...]
    up_scr[nh:nh + t_len] = a_ref[...] * _sigmoid(g_ref[...])
    for t in range(t_len):
        for c0 in range(0, c, col_blk):
            acc = jnp.zeros((bb, col_blk), F32) + bdw_ref[:, c0:c0 + col_blk]
            for k in range(CONV_K):
                acc = acc + up_scr[t + k, :, c0:c0 + col_blk] * w_ref[k:k + 1, c0:c0 + col_blk]
            cv_ref[t, :, c0:c0 + col_blk] = acc
    for t in range(t_len):
        v = cv_ref[t]
        mu = jnp.mean(v, axis=-1, keepdims=True)
        vc = v - mu
        var = jnp.mean(vc * vc, axis=-1, keepdims=True)
        y = vc * lax.rsqrt(var + EPS) * lng_ref[...] + lnb_ref[...]
        cv_ref[t] = y * _sigmoid(y)
    nb_ref[...] = up_scr[t_len:t_len + nh]


def _conv_tm_call(proj3, buf_t, w_dw, b_dw, ln_g, ln_b, c, col0, bb):
    t_len, bsz, _ = proj3.shape
    nh = CONV_K - 1
    ja, jg = col0 // c, col0 // c + 1
    vec = lambda: pl.BlockSpec((1, c), lambda i: (0, 0))
    return pl.pallas_call(
        functools.partial(_conv_tm_kernel, col_blk=512),
        grid=(bsz // bb,),
        in_specs=[pl.BlockSpec((t_len, bb, c), lambda i: (0, i, ja)),
                  pl.BlockSpec((t_len, bb, c), lambda i: (0, i, jg)),
                  pl.BlockSpec((nh, bb, c), lambda i: (0, i, 0)),
                  pl.BlockSpec((CONV_K, c), lambda i: (0, 0)),
                  vec(), vec(), vec()],
        out_specs=[pl.BlockSpec((t_len, bb, c), lambda i: (0, i, 0)),
                   pl.BlockSpec((nh, bb, c), lambda i: (0, i, 0))],
        out_shape=[jax.ShapeDtypeStruct((t_len, bsz, c), F32),
                   jax.ShapeDtypeStruct((nh, bsz, c), F32)],
        scratch_shapes=[pltpu.VMEM((nh + t_len, bb, c), F32)],
        compiler_params=_cparams("arbitrary"),
        name="conv_tm",
    )(proj3, proj3, buf_t, w_dw, b_dw.reshape(1, c), ln_g.reshape(1, c), ln_b.reshape(1, c))


def _post_mix_kernel(x_ref, o_ref, g1_ref, sh2_ref, sc2_ref, npm_ref, npf_ref, x1_ref, h2_ref):
    def chunk(lead, sub, flat):
        x1 = x_ref[lead, sub, :] + _mod_rows(g1_ref, sub) * _rms(o_ref[flat, :], npm_ref[...])
        x1_ref[lead, sub, :] = x1
        h2 = _rms(x1, npf_ref[...]) * (1.0 + _mod_rows(sc2_ref, sub)) + _mod_rows(sh2_ref, sub)
        h2_ref[flat, :] = h2.astype(BF16)

    _for_row_chunks(x_ref.shape, chunk)


def _post_mix_call(x3, o2d, mod, npm, npf):
    n_rows, d = o2d.shape
    tm = NORM_ROW_TILE
    row = lambda: pl.BlockSpec((tm, d), lambda i: (i, 0))
    vec = lambda: pl.BlockSpec((1, d), lambda i: (0, 0))
    return pl.pallas_call(
        _post_mix_kernel,
        grid=(n_rows // tm,),
        in_specs=[mod.x_spec(tm), row(), mod.spec(2, tm), mod.spec(3, tm), mod.spec(4, tm), vec(), vec()],
        out_specs=[mod.x_spec(tm), row()],
        out_shape=[jax.ShapeDtypeStruct(x3.shape, F32), jax.ShapeDtypeStruct((n_rows, d), BF16)],
        compiler_params=_cparams("arbitrary"),
        name="post_mix",
    )(x3, o2d, mod.arr, mod.arr, mod.arr, npm.reshape(1, d), npf.reshape(1, d))


def _final_kernel(x1_ref, ff_ref, g2_ref, n_ref, o_ref):
    def chunk(lead, sub, flat):
        o_ref[lead, sub, :] = x1_ref[lead, sub, :] + _mod_rows(g2_ref, sub) * _rms(ff_ref[flat, :], n_ref[...])

    _for_row_chunks(x1_ref.shape, chunk)


def _final_call(x1, ff_all, row0, mod, npost):
    n_rows, d = x1.shape[0] * x1.shape[1], x1.shape[2]
    tm = NORM_ROW_TILE
    i0 = row0 // tm
    return pl.pallas_call(
        _final_kernel,
        grid=(n_rows // tm,),
        in_specs=[mod.x_spec(tm),
                  pl.BlockSpec((tm, d), lambda i: (i0 + i, 0)),
                  mod.spec(5, tm),
                  pl.BlockSpec((1, d), lambda i: (0, 0))],
        out_specs=mod.x_spec(tm),
        out_shape=jax.ShapeDtypeStruct(x1.shape, F32),
        compiler_params=_cparams("arbitrary"),
        name="final",
    )(x1, ff_all, mod.arr, npost.reshape(1, d))


def _extract_top(s, n_top):
    rows = s.shape[0]
    rid = lax.broadcasted_iota(jnp.int32, s.shape, 0).astype(F32)
    rank = jnp.full(s.shape, float(rows), F32)
    vals = []
    for it in range(n_top):
        m = jnp.max(s, axis=0, keepdims=True)
        first = jnp.min(jnp.where(s == m, rid, float(rows)), axis=0, keepdims=True)
        sel = rid == first
        rank = jnp.where(sel, float(it), rank)
        s = jnp.where(sel, -jnp.inf, s)
        vals.append(m)
    return vals, rank


def _route_kernel(h2_ref, wq_ref, keys_ref, r1_ref, e1_ref, cnt_ref, e0_ref, cand_scr):
    nk = PK_NKEYS
    qt = lax.dot_general(wq_ref[...], h2_ref[...], (((1,), (1,)), ((), ())), preferred_element_type=F32)
    s0_all = jnp.dot(keys_ref[0, 0], qt[:nk], precision=HIGHEST, preferred_element_type=F32)
    s1_all = jnp.dot(keys_ref[0, 1], qt[nk:], precision=HIGHEST, preferred_element_type=F32)
    cand_row = lax.broadcasted_iota(jnp.int32, (PK_CAND_ROWS, LANES), 0)
    for c in range(h2_ref.shape[0] // LANES):
        sl = slice(c * LANES, (c + 1) * LANES)
        s0, s1 = s0_all[:, sl], s1_all[:, sl]
        a_vals, rank0 = _extract_top(s0, PK_TOPK)
        b_vals, rank1 = _extract_top(s1, PK_TOPK)
        cand_scr[...] = jnp.full(cand_scr.shape, -jnp.inf, F32)
        for r, (k0, k1) in enumerate(PK_CANDS):
            cand_scr[r:r + 1, :] = a_vals[k0] + b_vals[k1]
        best, rank2 = _extract_top(cand_scr[...], PK_TOPK)
        z = jnp.zeros_like(best[0])
        for v in best:
            z = z + jnp.exp(v - best[0])
        chosen = jnp.where(rank2 < float(PK_TOPK), 1.0, 0.0)
        cnt = jnp.zeros((nk, LANES), F32)
        r = 0
        for k0 in range(PK_TOPK):
            n_k0 = sum(1 for cand in PK_CANDS if cand[0] == k0)
            in_k0 = (cand_row >= r) & (cand_row < r + n_k0)
            cnt_k0 = jnp.sum(jnp.where(in_k0, chosen, 0.0), axis=0, keepdims=True)
            cnt = jnp.where(rank0 == float(k0), cnt_k0, cnt)
            r += n_k0
        r1_ref[0, :, sl] = rank1
        e1_ref[0, :, sl] = jnp.exp(s1 - b_vals[0])
        cnt_ref[0, :, sl] = cnt
        e0_ref[0, :, sl] = jnp.exp(s0 - a_vals[0]) / z


def _route_call(h2_all, wqt_b, keys):
    n_tok, d = h2_all.shape
    heads, _, nk, half = keys.shape
    tt = 256
    out = lambda: pl.BlockSpec((1, nk, tt), lambda i, h: (h, 0, i))
    return pl.pallas_call(
        _route_kernel,
        grid=(n_tok // tt, heads),
        in_specs=[pl.BlockSpec((tt, d), lambda i, h: (i, 0)),
                  pl.BlockSpec((2 * half, d), lambda i, h: (h, 0)),
                  pl.BlockSpec((1, 2, nk, half), lambda i, h: (h, 0, 0, 0))],
        out_specs=[out(), out(), out(), out()],
        out_shape=[jax.ShapeDtypeStruct((heads, nk, n_tok), F32)] * 4,
        scratch_shapes=[pltpu.VMEM((PK_CAND_ROWS, LANES), F32)],
        compiler_params=_cparams("arbitrary", "arbitrary"),
        name="peer_route",
    )(h2_all, wqt_b, keys)


def _experts_kernel(h2_ref, u_ref, v_ref, r1_ref, e1_ref, cnt_ref, e0_ref, o_ref, act_scr, w_scr):
    eb, tt = act_scr.shape
    nk = PK_NKEYS
    e = pl.program_id(1)

    @pl.when(e == 0)
    def _():
        o_ref[...] = jnp.zeros_like(o_ref)

    st = lax.dot_general(u_ref[...], h2_ref[...], (((1,), (1,)), ((), ())), preferred_element_type=F32)
    act_scr[...] = _gelu_tanh(st)
    for ii in range(eb // nk):
        i_glob = e * (eb // nk) + ii
        cnt_rows = [cnt_ref[h, pl.ds(i_glob, 1), :] for h in range(PK_HEADS)]
        e0_rows = [e0_ref[h, pl.ds(i_glob, 1), :] for h in range(PK_HEADS)]
        for c in range(tt // LANES):
            sl = slice(c * LANES, (c + 1) * LANES)
            w = jnp.zeros((nk, LANES), F32)
            for h in range(PK_HEADS):
                w = w + jnp.where(r1_ref[h, :, sl] < cnt_rows[h][:, sl],
                                  e1_ref[h, :, sl] * e0_rows[h][:, sl], 0.0)
            w_scr[ii * nk:(ii + 1) * nk, sl] = (w * act_scr[ii * nk:(ii + 1) * nk, sl]).astype(BF16)
    o_ref[...] += lax.dot_general(w_scr[...], v_ref[...], (((0,), (0,)), ((), ())),
                                  preferred_element_type=F32)


def _experts_call(h2_all, u_b, v_b, r1, e1, cnt, e0):
    n_tok, d = h2_all.shape
    n_exp = u_b.shape[0]
    heads, nk, _ = r1.shape
    tt, eb = 512, 512
    once = pl.Buffered(1)
    aux = lambda: pl.BlockSpec((heads, nk, tt), lambda i, e: (0, 0, i), pipeline_mode=once)
    return pl.pallas_call(
        _experts_kernel,
        grid=(n_tok // tt, n_exp // eb),
        in_specs=[pl.BlockSpec((tt, d), lambda i, e: (i, 0), pipeline_mode=once),
                  pl.BlockSpec((eb, d), lambda i, e: (e, 0)),
                  pl.BlockSpec((eb, d), lambda i, e: (e, 0)),
                  aux(), aux(), aux(), aux()],
        out_specs=pl.BlockSpec((tt, d), lambda i, e: (i, 0)),
        out_shape=jax.ShapeDtypeStruct((n_tok, d), F32),
        scratch_shapes=[pltpu.VMEM((eb, tt), F32), pltpu.VMEM((eb, tt), BF16)],
        compiler_params=_cparams("arbitrary", "arbitrary"),
        name="peer_experts",
    )(h2_all, u_b, v_b, r1, e1, cnt, e0)


def _mixer(x3, mod, h0_re, h0_im, conv_buf, w):
    d = x3.shape[2]
    n_rows = x3.shape[0] * x3.shape[1]
    proj = _inproj_call(x3, mod, w["norm_pre_mix"], w["w_in_b"])
    proj3 = proj.reshape(x3.shape[0], x3.shape[1], -1)
    sw = w["ssm_b_glu"].shape[0]
    cw = w["conv_b_dw"].shape[0]

    y_ssm, new_re, new_im = _ssm_branch(proj3, mod.time_major, h0_re, h0_im, w["ssm"])
    z = _glu_call(y_ssm, w["w_glu_b"], w["ssm_b_glu"])
    y_a = _gated_proj_call(z, w["w_proj_b"], proj, sw + 2 * cw)

    conv_w = (w["conv_w_dw"], w["conv_b_dw"], w["conv_ln_g"], w["conv_ln_b"], cw, sw)
    if mod.time_major:
        cv, nb_t = _conv_tm_call(proj3, jnp.transpose(conv_buf, (1, 0, 2)), *conv_w, 16)
        new_buf = jnp.transpose(nb_t, (1, 0, 2))
    else:
        if conv_buf is None:
            conv_buf = jnp.zeros((x3.shape[0], CONV_K - 1, cw), F32)
        cv, new_buf = _conv_call(proj3, conv_buf, *conv_w, 1, 256)
    mix = _gated_proj_call(cv.reshape(n_rows, cw), w["w_pw_b"], proj, sw + 2 * cw + d, add=y_a, out_dtype=BF16)
    o = _matmul_call(mix, w["w_out_b"])
    x1, h2 = _post_mix_call(x3, o, mod, w["norm_post_mix"], w["norm_pre_ffn"])
    return x1, h2, new_re, new_im, new_buf


def _layer(xp, xs, c_prompt, c_sample, st_re, st_im, st_conv, w):
    bp, tp, d = xp.shape
    bs, ts, _ = xs.shape

    pad = (-(bp + bs)) % 8
    c_all = jnp.concatenate([c_prompt, c_sample, jnp.zeros((pad, d), F32)], axis=0)
    mod_all = _mod_call(c_all, w["w_mod"], w["b_mod"])
    mod_p = _Mod(mod_all[:bp].reshape(bp, 1, N_MOD * d), False, tp, d)
    mod_s = _Mod(mod_all[bp:bp + bs], True, bs, d)

    x1p, h2p, p_re, p_im, p_buf = _mixer(xp, mod_p, None, None, None, w)
    x1s, h2s, s_re, s_im, s_buf = _mixer(jnp.transpose(xs, (1, 0, 2)), mod_s, st_re, st_im, st_conv, w)

    h2_all = jnp.concatenate([h2p, h2s], axis=0)
    r1, e1, cnt, e0 = _route_call(h2_all, w["w_qt_b"], w["peer_sub_keys"])
    ff = _experts_call(h2_all, w["peer_u_b"], w["peer_v_b"], r1, e1, cnt, e0)

    yp = _final_call(x1p, ff, 0, mod_p, w["norm_post_ffn"])
    ys = jnp.transpose(_final_call(x1s, ff, bp * tp, mod_s, w["norm_post_ffn"]), (1, 0, 2))
    return yp, ys, p_re, p_im, p_buf, s_re, s_im, s_buf


def kernel(x_prompt, x_sample, c_prompt, c_sample, state_ssm_re, state_ssm_im, state_conv, w_mod, b_mod,
           norm_pre_mix, norm_post_mix, norm_pre_ffn, norm_post_ffn, w_in, ssm_lambda_re, ssm_lambda_im,
           ssm_log_dt, ssm_b_re, ssm_b_im, ssm_c_re, ssm_c_im, ssm_d, ssm_w_glu, ssm_b_glu, w_proj_ssm,
           conv_w_dw, conv_b_dw, conv_ln_g, conv_ln_b, conv_w_pw, w_out, peer_w_q, peer_sub_keys, peer_u,
           peer_v):
    depth = w_mod.shape[0]
    xp, xs = x_prompt, x_sample
    outs = [[] for _ in range(6)]
    for l in range(depth):
        w = {
            "w_mod": w_mod[l], "b_mod": b_mod[l],
            "norm_pre_mix": norm_pre_mix[l], "norm_post_mix": norm_post_mix[l],
            "norm_pre_ffn": norm_pre_ffn[l], "norm_post_ffn": norm_post_ffn[l],
            "w_in_b": w_in[l].astype(BF16),
            "ssm": (ssm_lambda_re[l], ssm_lambda_im[l], ssm_log_dt[l], ssm_b_re[l], ssm_b_im[l],
                    ssm_c_re[l], ssm_c_im[l], ssm_d[l]),
            "w_glu_b": ssm_w_glu[l].astype(BF16), "ssm_b_glu": ssm_b_glu[l],
            "w_proj_b": w_proj_ssm[l].astype(BF16),
            "conv_w_dw": conv_w_dw[l], "conv_b_dw": conv_b_dw[l],
            "conv_ln_g": conv_ln_g[l], "conv_ln_b": conv_ln_b[l],
            "w_pw_b": conv_w_pw[l].astype(BF16), "w_out_b": w_out[l].astype(BF16),
            "w_qt_b": jnp.transpose(peer_w_q[l]).astype(BF16),
            "peer_sub_keys": peer_sub_keys[l],
            "peer_u_b": peer_u[l].astype(BF16), "peer_v_b": peer_v[l].astype(BF16),
        }
        xp, xs, *states = _layer(xp, xs, c_prompt, c_sample, state_ssm_re[l], state_ssm_im[l], state_conv[l], w)
        for acc, s in zip(outs, states):
            acc.append(s)
    return (xp, xs) + tuple(jnp.stack(o) for o in outs)
```

```python
import functools
import math

import jax
import jax.numpy as jnp
from jax import lax
from jax.experimental import pallas as pl
from jax.experimental.pallas import tpu as pltpu

F32 = jnp.float32
BF16 = jnp.bfloat16
HIGHEST = lax.Precision.HIGHEST

EPS = 1e-6
N_MOD = 6
SSM_GROUP = 16
SSM_STATE = 64
CONV_K = 31
CONV_HIST = 32
PK_HEADS = 8
PK_NKEYS = 128
PK_TOPK = 16
SSM_CHUNK = 16

V7X_VMEM_BYTES = 64 * 1024 * 1024
VMEM_CAP_MB = 56
LANES = 128
ROW_TILE = 512
NORM_ROW_TILE = 256
COL_TILE = 1024

PK_CANDS = tuple((k0, k1) for k0 in range(PK_TOPK) for k1 in range(PK_TOPK)
                 if (k0 + 1) * (k1 + 1) <= PK_TOPK)
PK_CAND_ROWS = 56


def _cparams(*sem):
    assert VMEM_CAP_MB * 1024 * 1024 <= V7X_VMEM_BYTES
    return pltpu.CompilerParams(dimension_semantics=sem, vmem_limit_bytes=VMEM_CAP_MB * 1024 * 1024)


def _sigmoid(x):
    return 1.0 / (1.0 + jnp.exp(-x))


def _gelu_tanh(x):
    c = math.sqrt(2.0 / math.pi)
    return 0.5 * x * (1.0 + jnp.tanh(c * (x + 0.044715 * (x * x * x))))


def _rms(x, g):
    return x * lax.rsqrt(jnp.mean(x * x, axis=-1, keepdims=True) + EPS) * g


def _mod_kernel(c_ref, w_ref, b_ref, o_ref):
    c = c_ref[...]
    a = (c * _sigmoid(c)).astype(BF16)
    o_ref[...] = jnp.dot(a, w_ref[...].astype(BF16), preferred_element_type=F32) + b_ref[...]


def _mod_call(c_all, w_mod, b_mod):
    m, d = c_all.shape
    n = w_mod.shape[1]
    tn = 512
    return pl.pallas_call(
        _mod_kernel,
        grid=(n // tn,),
        in_specs=[pl.BlockSpec((m, d), lambda j: (0, 0)),
                  pl.BlockSpec((d, tn), lambda j: (0, j)),
                  pl.BlockSpec((1, tn), lambda j: (0, j))],
        out_specs=pl.BlockSpec((m, tn), lambda j: (0, j)),
        out_shape=jax.ShapeDtypeStruct((m, n), F32),
        compiler_params=_cparams("arbitrary"),
        name="mod",
    )(c_all, w_mod, b_mod.reshape(1, n))


class _Mod:
    def __init__(self, arr, time_major, seq, d):
        self.arr, self.time_major, self.seq, self.d = arr, time_major, seq, d

    def x_spec(self, rows):
        if self.time_major:
            assert rows % self.seq == 0
            return pl.BlockSpec((rows // self.seq, self.seq, self.d), lambda i, *_: (i, 0, 0))
        assert self.seq % rows == 0
        tpb = self.seq // rows
        return pl.BlockSpec((1, rows, self.d), lambda i, *_: (i // tpb, i % tpb, 0))

    def spec(self, k, rows):
        if self.time_major:
            return pl.BlockSpec((self.seq, self.d), lambda i, *_: (0, k))
        tpb = self.seq // rows
        return pl.BlockSpec((None, 1, self.d), lambda i, *_: (i // tpb, 0, k))


NORM_CHUNK_ROWS = 64


def _for_row_chunks(tile_shape, fn):
    n_lead, n_rows, _ = tile_shape
    cr = NORM_CHUNK_ROWS
    per_lead = n_rows // cr

    def body(c, carry):
        sub = pl.ds(pl.multiple_of((c % per_lead) * cr, cr), cr)
        fn(c // per_lead, sub, pl.ds(pl.multiple_of(c * cr, cr), cr))
        return carry

    lax.fori_loop(0, n_lead * per_lead, body, 0)


def _mod_rows(ref, sub):
    return ref[...] if ref.shape[0] == 1 else ref[sub, :]


def _inproj_kernel(x_ref, sh_ref, sc_ref, g_ref, w_ref, o_ref, h_scr):
    @pl.when(pl.program_id(1) == 0)
    def _():
        def chunk(lead, sub, flat):
            h = _rms(x_ref[lead, sub, :], g_ref[...]) * (1.0 + _mod_rows(sc_ref, sub)) + _mod_rows(sh_ref, sub)
            h_scr[flat, :] = h.astype(BF16)

        _for_row_chunks(x_ref.shape, chunk)

    o_ref[...] = jnp.dot(h_scr[...], w_ref[...], preferred_element_type=F32)


def _inproj_call(x3, mod, g, w_b):
    n_rows, d = x3.shape[0] * x3.shape[1], x3.shape[2]
    n = w_b.shape[1]
    tm, tn = ROW_TILE, COL_TILE
    return pl.pallas_call(
        _inproj_kernel,
        grid=(n_rows // tm, n // tn),
        in_specs=[mod.x_spec(tm), mod.spec(0, tm), mod.spec(1, tm),
                  pl.BlockSpec((1, d), lambda i, j: (0, 0)),
                  pl.BlockSpec((d, tn), lambda i, j: (0, j))],
        out_specs=pl.BlockSpec((tm, tn), lambda i, j: (i, j)),
        out_shape=jax.ShapeDtypeStruct((n_rows, n), F32),
        scratch_shapes=[pltpu.VMEM((tm, d), BF16)],
        compiler_params=_cparams("arbitrary", "arbitrary"),
        name="inproj",
    )(x3, mod.arr, mod.arr, g.reshape(1, d), w_b)


def _glu_kernel(y_ref, yt_ref, w_ref, b_ref, o_ref, a_scr):
    @pl.when(pl.program_id(1) == 0)
    def _():
        a_scr[...] = _gelu_tanh(y_ref[...]).astype(BF16)

    acc = jnp.dot(a_scr[...], w_ref[...], preferred_element_type=F32) + b_ref[...]
    o_ref[...] = (_gelu_tanh(yt_ref[...]) * _sigmoid(acc)).astype(BF16)


def _glu_call(y2d, w_b, b):
    n_rows, k = y2d.shape
    n = w_b.shape[1]
    tm, tn = ROW_TILE, COL_TILE
    return pl.pallas_call(
        _glu_kernel,
        grid=(n_rows // tm, n // tn),
        in_specs=[pl.BlockSpec((tm, k), lambda i, j: (i, 0)),
                  pl.BlockSpec((tm, tn), lambda i, j: (i, j)),
                  pl.BlockSpec((k, tn), lambda i, j: (0, j)),
                  pl.BlockSpec((1, tn), lambda i, j: (0, j))],
        out_specs=pl.BlockSpec((tm, tn), lambda i, j: (i, j)),
        out_shape=jax.ShapeDtypeStruct((n_rows, n), BF16),
        scratch_shapes=[pltpu.VMEM((tm, k), BF16)],
        compiler_params=_cparams("arbitrary", "arbitrary"),
        name="ssm_glu",
    )(y2d, y2d, w_b, b.reshape(1, n))


def _gated_proj_kernel(z_ref, w_ref, gate_ref, o_ref):
    acc = jnp.dot(z_ref[...].astype(BF16), w_ref[...], preferred_element_type=F32)
    o_ref[...] = _sigmoid(gate_ref[...]) * acc


def _gated_add_proj_kernel(z_ref, w_ref, gate_ref, a_ref, o_ref):
    acc = jnp.dot(z_ref[...].astype(BF16), w_ref[...], preferred_element_type=F32)
    o_ref[...] = (a_ref[...] + _sigmoid(gate_ref[...]) * acc).astype(o_ref.dtype)


def _gated_proj_call(z2d, w_b, proj, gate_col0, add=None, out_dtype=F32):
    n_rows, k = z2d.shape
    n = w_b.shape[1]
    tm, tn = ROW_TILE, COL_TILE
    gj = gate_col0 // tn
    in_specs = [pl.BlockSpec((tm, k), lambda i, j: (i, 0)),
                pl.BlockSpec((k, tn), lambda i, j: (0, j)),
                pl.BlockSpec((tm, tn), lambda i, j: (i, gj + j))]
    args = [z2d, w_b, proj]
    if add is not None:
        in_specs.append(pl.BlockSpec((tm, tn), lambda i, j: (i, j)))
        args.append(add)
    return pl.pallas_call(
        _gated_proj_kernel if add is None else _gated_add_proj_kernel,
        grid=(n_rows // tm, n // tn),
        in_specs=in_specs,
        out_specs=pl.BlockSpec((tm, tn), lambda i, j: (i, j)),
        out_shape=jax.ShapeDtypeStruct((n_rows, n), out_dtype),
        compiler_params=_cparams("arbitrary", "arbitrary"),
        name="gated_proj" if add is None else "gated_add_proj",
    )(*args)


def _matmul_kernel(a_ref, w_ref, o_ref):
    o_ref[...] = jnp.dot(a_ref[...], w_ref[...], preferred_element_type=F32)


def _matmul_call(a2d, w_b):
    n_rows, k = a2d.shape
    n = w_b.shape[1]
    tm, tn = ROW_TILE, COL_TILE
    return pl.pallas_call(
        _matmul_kernel,
        grid=(n_rows // tm, n // tn),
        in_specs=[pl.BlockSpec((tm, k), lambda i, j: (i, 0)),
                  pl.BlockSpec((k, tn), lambda i, j: (0, j))],
        out_specs=pl.BlockSpec((tm, tn), lambda i, j: (i, j)),
        out_shape=jax.ShapeDtypeStruct((n_rows, n), F32),
        compiler_params=_cparams("arbitrary", "arbitrary"),
        name="out_proj",
    )(a2d, w_b)


def _ssm_param_kernel(lrc_ref, lic_ref, lrr_ref, lir_ref, ldt_ref, ctre_ref, ctim_ref, btre_ref, btim_ref,
                      dpad_ref, t_ref, sre_ref, sim_ref, rre_ref, rim_ref, lre_ref, lim_ref, *, chunk):
    p, n = SSM_GROUP, SSM_STATE
    lp = chunk * p
    sre_ref[...] = jnp.zeros_like(sre_ref)
    sim_ref[...] = jnp.zeros_like(sim_ref)
    rre_ref[...] = jnp.zeros_like(rre_ref)
    rim_ref[...] = jnp.zeros_like(rim_ref)
    lane = lax.broadcasted_iota(jnp.int32, (p, lp), 1)
    diag = lax.broadcasted_iota(jnp.int32, (p, lp), 0) == lane
    k_lane = (lax.broadcasted_iota(jnp.int32, (n, lp), 1) // p).astype(F32)
    e_row = (chunk - 1 - lax.broadcasted_iota(jnp.int32, (chunk, n), 0)).astype(F32)
    for a in range(2):
        dt = jnp.exp(ldt_ref[a])
        lrc, lic = lrc_ref[a] * dt, lic_ref[a] * dt
        lrr, lir = lrr_ref[a], lir_ref[a]
        m0, m1 = jnp.exp(lrc * k_lane), jnp.exp(lrc * (k_lane + 1.0))
        p0re, p0im = m0 * jnp.cos(lic * k_lane), m0 * jnp.sin(lic * k_lane)
        p1re, p1im = m1 * jnp.cos(lic * (k_lane + 1.0)), m1 * jnp.sin(lic * (k_lane + 1.0))
        ctre, ctim = ctre_ref[a], ctim_ref[a]
        dec = jnp.exp(lrr * dt)
        abre, abim = dec * jnp.cos(lir * dt), dec * jnp.sin(lir * dt)
        den = lrr * lrr + lir * lir
        fre = ((abre - 1.0) * lrr + abim * lir) / den
        fim = (abim * lrr - (abre - 1.0) * lir) / den
        bre = fre * btre_ref[a] - fim * btim_ref[a]
        bim = fre * btim_ref[a] + fim * btre_ref[a]
        clre = ctre * p0re - ctim * p0im
        clim = ctre * p0im + ctim * p0re
        kt = (jnp.dot(bre, clre, precision=HIGHEST, preferred_element_type=F32)
              - jnp.dot(bim, clim, precision=HIGHEST, preferred_element_type=F32))
        kt = kt + jnp.where(diag, dpad_ref[a], 0.0)
        for s in range(chunk):
            blk = kt if s == 0 else jnp.where(lane >= s * p, pltpu.roll(kt, s * p, 1), 0.0)
            t_ref[a, s * p:(s + 1) * p, :] = blk.astype(BF16)
        mr = jnp.exp(lrr * dt * e_row)
        prre, prim = mr * jnp.cos(lir * dt * e_row), mr * jnp.sin(lir * dt * e_row)
        for s in range(chunk):
            pr, pi = prre[s:s + 1, :], prim[s:s + 1, :]
            r0 = a * lp + s * p
            sre_ref[0, r0:r0 + p, a * n:(a + 1) * n] = bre * pr - bim * pi
            sim_ref[0, r0:r0 + p, a * n:(a + 1) * n] = bre * pi + bim * pr
        rre_ref[0, a * n:(a + 1) * n, a * lp:(a + 1) * lp] = (ctre * p1re - ctim * p1im).astype(BF16)
        rim_ref[0, a * n:(a + 1) * n, a * lp:(a + 1) * lp] = (-(ctre * p1im + ctim * p1re)).astype(BF16)
        ml = jnp.exp(lrr * dt * float(chunk))
        lre_ref[0, :, a * n:(a + 1) * n] = ml * jnp.cos(lir * dt * float(chunk))
        lim_ref[0, :, a * n:(a + 1) * n] = ml * jnp.sin(lir * dt * float(chunk))


def _ssm_param_call(lam_re, lam_im, log_dt, b_re, b_im, c_re, c_im, d_skip, chunk):
    g, n = lam_re.shape
    p = SSM_GROUP
    lp = chunk * p
    ct_re = jnp.tile(jnp.transpose(c_re, (0, 2, 1)), (1, 1, chunk))
    ct_im = jnp.tile(jnp.transpose(c_im, (0, 2, 1)), (1, 1, chunk))
    bt_re = jnp.transpose(b_re, (0, 2, 1))
    bt_im = jnp.transpose(b_im, (0, 2, 1))
    dpad = jnp.pad(d_skip.reshape(g, 1, p), ((0, 0), (0, 0), (0, lp - p)))
    spec3 = lambda s: pl.BlockSpec((2,) + s, lambda i: (i, 0, 0))
    pair = lambda s: pl.BlockSpec((1,) + s, lambda i: (i, 0, 0))
    return pl.pallas_call(
        functools.partial(_ssm_param_kernel, chunk=chunk),
        grid=(g // 2,),
        in_specs=[spec3((n, 1)), spec3((n, 1)), spec3((1, n)), spec3((1, n)), spec3((1, 1)),
                  spec3((n, lp)), spec3((n, lp)), spec3((p, n)), spec3((p, n)), spec3((1, lp))],
        out_specs=[spec3((lp, lp)), pair((2 * lp, 2 * n)), pair((2 * lp, 2 * n)),
                   pair((2 * n, 2 * lp)), pair((2 * n, 2 * lp)), pair((1, 2 * n)), pair((1, 2 * n))],
        out_shape=[jax.ShapeDtypeStruct((g, lp, lp), BF16),
                   jax.ShapeDtypeStruct((g // 2, 2 * lp, 2 * n), F32),
                   jax.ShapeDtypeStruct((g // 2, 2 * lp, 2 * n), F32),
                   jax.ShapeDtypeStruct((g // 2, 2 * n, 2 * lp), BF16),
                   jax.ShapeDtypeStruct((g // 2, 2 * n, 2 * lp), BF16),
                   jax.ShapeDtypeStruct((g // 2, 1, 2 * n), F32),
                   jax.ShapeDtypeStruct((g // 2, 1, 2 * n), F32)],
        compiler_params=_cparams("arbitrary"),
        name="ssm_params",
    )(lam_re.reshape(g, n, 1), lam_im.reshape(g, n, 1), lam_re.reshape(g, 1, n), lam_im.reshape(g, 1, n),
      log_dt.reshape(g, 1, 1), ct_re, ct_im, bt_re, bt_im, dpad)


SSM_LANE_GROUPS = LANES // SSM_GROUP
SSM_ROW_BLK = 32


def _transpose_lane_blocks(load_src, store_dst):
    n = SSM_LANE_GROUPS
    piece = load_src(0)
    lane_blk = lax.broadcasted_iota(jnp.int32, piece.shape, 1) // SSM_GROUP
    acc = [None] * n
    for src in range(n):
        if src:
            piece = load_src(src)
        for dst in range(n):
            shift = ((src - dst) * SSM_GROUP) % LANES
            moved = piece if shift == 0 else pltpu.roll(piece, shift, 1)
            acc[dst] = moved if src == 0 else jnp.where(lane_blk == src, moved, acc[dst])
    for dst in range(n):
        store_dst(dst, acc[dst])


def _ssm_main_kernel(u_ref, t_ref, sre_ref, sim_ref, rre_ref, rim_ref, lre_ref, lim_ref, h0re_ref, h0im_ref,
                     y_ref, hnre_ref, hnim_ref, x_scr, yg_scr, vre_scr, vim_scr, hre_scr, him_scr,
                     *, batch, chunks, chunk, time_major):
    rows = batch * chunks
    lp = chunk * SSM_GROUP
    rb = SSM_ROW_BLK

    def token_rows(r0, step):
        if time_major:
            return pl.ds(step * batch + r0, rb)
        return pl.ds(r0 * chunk + step, rb, stride=chunk)

    def regroup(i, carry):
        r0 = pl.multiple_of(i * rb, rb)
        for cb in range(lp // LANES):
            lanes = slice(cb * LANES, (cb + 1) * LANES)
            steps = cb * SSM_LANE_GROUPS

            def store_x(g, val):
                x_scr[g, pl.ds(r0, rb), lanes] = val

            _transpose_lane_blocks(lambda s8: u_ref[token_rows(r0, steps + s8), :], store_x)
        return carry

    lax.fori_loop(0, rows // rb, regroup, 0)

    def pair(pr, carry):
        u0, u1 = x_scr[2 * pr], x_scr[2 * pr + 1]
        ucat = jnp.concatenate([u0, u1], axis=1)
        vre_scr[...] = jnp.dot(ucat, sre_ref[pr], precision=HIGHEST, preferred_element_type=F32)
        vim_scr[...] = jnp.dot(ucat, sim_ref[pr], precision=HIGHEST, preferred_element_type=F32)
        are, aim = lre_ref[pr], lim_ref[pr]
        if chunks == 1:
            hre, him = h0re_ref[pr], h0im_ref[pr]
            hre_scr[...] = hre
            him_scr[...] = him
            hnre_ref[pr] = are * hre - aim * him + vre_scr[...]
            hnim_ref[pr] = are * him + aim * hre + vim_scr[...]
        else:
            init = (tuple(h0re_ref[pr, b:b + 1, :] for b in range(batch))
                    + tuple(h0im_ref[pr, b:b + 1, :] for b in range(batch)))

            def body(m, hs):
                nre, nim = [], []
                for b in range(batch):
                    hr, hi = hs[b], hs[batch + b]
                    row = b * chunks + m
                    hre_scr[pl.ds(row, 1), :] = hr
                    him_scr[pl.ds(row, 1), :] = hi
                    nre.append(are * hr - aim * hi + vre_scr[pl.ds(row, 1), :])
                    nim.append(are * hi + aim * hr + vim_scr[pl.ds(row, 1), :])
                return tuple(nre) + tuple(nim)

            fin = lax.fori_loop(0, chunks, body, init)
            for b in range(batch):
                hnre_ref[pr, b:b + 1, :] = fin[b]
                hnim_ref[pr, b:b + 1, :] = fin[batch + b]
        yh = (jnp.dot(hre_scr[...].astype(BF16), rre_ref[pr], preferred_element_type=F32)
              + jnp.dot(him_scr[...].astype(BF16), rim_ref[pr], preferred_element_type=F32))
        yg_scr[2 * pr] = jnp.dot(u0.astype(BF16), t_ref[2 * pr], preferred_element_type=F32) + yh[:, :lp]
        yg_scr[2 * pr + 1] = jnp.dot(u1.astype(BF16), t_ref[2 * pr + 1], preferred_element_type=F32) + yh[:, lp:]
        return carry

    lax.fori_loop(0, SSM_LANE_GROUPS // 2, pair, 0)

    def ungroup(i, carry):
        r0 = pl.multiple_of(i * rb, rb)
        for cb in range(lp // LANES):
            lanes = slice(cb * LANES, (cb + 1) * LANES)
            steps = cb * SSM_LANE_GROUPS

            def store_y(t8, val):
                y_ref[token_rows(r0, steps + t8), :] = val

            _transpose_lane_blocks(lambda g: yg_scr[g, pl.ds(r0, rb), lanes], store_y)
        return carry

    lax.fori_loop(0, rows // rb, ungroup, 0)


def _ssm_main_call(proj, ops, h0re, h0im, batch, chunks, chunk, time_major):
    t_op, s_re, s_im, r_re, r_im, l_re, l_im = ops
    n_tok = proj.shape[0]
    g, lp, _ = t_op.shape
    rows = batch * chunks
    assert rows % SSM_ROW_BLK == 0
    n2 = 2 * SSM_STATE
    ng = SSM_LANE_GROUPS
    grp = lambda s: pl.BlockSpec((ng,) + s, lambda j: (j, 0, 0))
    pair = lambda s: pl.BlockSpec((ng // 2,) + s, lambda j: (j, 0, 0))
    tok = lambda: pl.BlockSpec((n_tok, LANES), lambda j: (0, j))
    return pl.pallas_call(
        functools.partial(_ssm_main_kernel, batch=batch, chunks=chunks, chunk=chunk, time_major=time_major),
        grid=(g // ng,),
        in_specs=[tok(), grp((lp, lp)), pair((2 * lp, n2)), pair((2 * lp, n2)),
                  pair((n2, 2 * lp)), pair((n2, 2 * lp)), pair((1, n2)), pair((1, n2)),
                  pair((batch, n2)), pair((batch, n2))],
        out_specs=[tok(), pair((batch, n2)), pair((batch, n2))],
        out_shape=[jax.ShapeDtypeStruct((n_tok, g * SSM_GROUP), F32),
                   jax.ShapeDtypeStruct((g // 2, batch, n2), F32),
                   jax.ShapeDtypeStruct((g // 2, batch, n2), F32)],
        scratch_shapes=[pltpu.VMEM((ng, rows, lp), F32)] * 2 + [pltpu.VMEM((rows, n2), F32)] * 4,
        compiler_params=_cparams("arbitrary"),
        name="ssm_main",
    )(proj, t_op, s_re, s_im, r_re, r_im, l_re, l_im, h0re, h0im)


def _ssm_branch(proj, bsz, t_len, time_major, h0_re, h0_im, ssm_w):
    lam_re, lam_im, log_dt, b_re, b_im, c_re, c_im, d_skip = ssm_w
    g, n = lam_re.shape
    chunk = min(SSM_CHUNK, t_len)
    assert t_len % chunk == 0 and (chunk * SSM_GROUP) % LANES == 0
    chunks = t_len // chunk
    assert not time_major or chunks == 1
    ops = _ssm_param_call(lam_re, lam_im, log_dt, b_re, b_im, c_re, c_im, d_skip, chunk)
    if h0_re is None:
        h0re = jnp.zeros((g // 2, bsz, 2 * n), F32)
        h0im = h0re
    else:
        h0re = jnp.transpose(h0_re.reshape(bsz, g // 2, 2 * n), (1, 0, 2))
        h0im = jnp.transpose(h0_im.reshape(bsz, g // 2, 2 * n), (1, 0, 2))
    y, hn_re, hn_im = _ssm_main_call(proj, ops, h0re, h0im, bsz, chunks, chunk, time_major)
    new_re = jnp.transpose(hn_re, (1, 0, 2)).reshape(bsz, g, n)
    new_im = jnp.transpose(hn_im, (1, 0, 2)).reshape(bsz, g, n)
    return y, new_re, new_im


def _conv_kernel(a_ref, g_ref, buf_ref, w_ref, bdw_ref, lng_ref, lnb_ref, cv_ref, nb_ref, up_scr, acc_scr,
                 *, row_blk, col_blk):
    bb, tt, c = a_ref.shape
    hist = CONV_HIST
    off = hist - (CONV_K - 1)
    j = pl.program_id(1)

    @pl.when(j == 0)
    def _():
        up_scr[:, off:hist, :] = buf_ref[...]

    up_scr[:, hist:hist + tt, :] = a_ref[...] * _sigmoid(g_ref[...])
    for r0 in range(0, tt, row_blk):
        for c0 in range(0, c, col_blk):
            acc = jnp.zeros((bb, row_blk, col_blk), F32) + bdw_ref[:, c0:c0 + col_blk][None]
            for k in range(CONV_K):
                tap = up_scr[:, r0 + k + off:r0 + k + off + row_blk, c0:c0 + col_blk]
                acc = acc + tap * w_ref[k:k + 1, c0:c0 + col_blk][None]
            acc_scr[:, r0:r0 + row_blk, c0:c0 + col_blk] = acc
    v = acc_scr[...]
    mu = jnp.mean(v, axis=-1, keepdims=True)
    vc = v - mu
    var = jnp.mean(vc * vc, axis=-1, keepdims=True)
    y = vc * lax.rsqrt(var + EPS) * lng_ref[...][None] + lnb_ref[...][None]
    cv_ref[...] = y * _sigmoid(y)
    tail = up_scr[:, tt + off:tt + hist, :]

    @pl.when(j == pl.num_programs(1) - 1)
    def _():
        nb_ref[...] = tail

    up_scr[:, off:hist, :] = tail


def _conv_call(proj3, buf, w_dw, b_dw, ln_g, ln_b, c, col0, bb, tt):
    bsz, t_len, _ = proj3.shape
    ja, jg = col0 // c, col0 // c + 1
    row_blk = min(tt, 64)
    vec = lambda: pl.BlockSpec((1, c), lambda i, j: (0, 0))
    return pl.pallas_call(
        functools.partial(_conv_kernel, row_blk=row_blk, col_blk=256),
        grid=(bsz // bb, t_len // tt),
        in_specs=[pl.BlockSpec((bb, tt, c), lambda i, j: (i, j, ja)),
                  pl.BlockSpec((bb, tt, c), lambda i, j: (i, j, jg)),
                  pl.BlockSpec((bb, CONV_K - 1, c), lambda i, j: (i, 0, 0)),
                  pl.BlockSpec((CONV_K, c), lambda i, j: (0, 0)),
                  vec(), vec(), vec()],
        out_specs=[pl.BlockSpec((bb, tt, c), lambda i, j: (i, j, 0)),
                   pl.BlockSpec((bb, CONV_K - 1, c), lambda i, j: (i, 0, 0))],
        out_shape=[jax.ShapeDtypeStruct((bsz, t_len, c), F32),
                   jax.ShapeDtypeStruct((bsz, CONV_K - 1, c), F32)],
        scratch_shapes=[pltpu.VMEM((bb, CONV_HIST + tt, c), F32), pltpu.VMEM((bb, tt, c), F32)],
        compiler_params=_cparams("arbitrary", "arbitrary"),
        name="conv",
    )(proj3, proj3, buf, w_dw, b_dw.reshape(1, c), ln_g.reshape(1, c), ln_b.reshape(1, c))


def _conv_tm_kernel(a_ref, g_ref, buf_ref, w_ref, bdw_ref, lng_ref, lnb_ref, cv_ref, nb_ref, up_scr, *, col_blk):
    t_len, bb, c = a_ref.shape
    nh = CONV_K - 1
    up_scr[0:nh] = buf_ref[...]
    up_scr[nh:nh + t_len] = a_ref[...] * _sigmoid(g_ref[...])
    for t in range(t_len):
        for c0 in range(0, c, col_blk):
            acc = jnp.zeros((bb, col_blk), F32) + bdw_ref[:, c0:c0 + col_blk]
            for k in range(CONV_K):
                acc = acc + up_scr[t + k, :, c0:c0 + col_blk] * w_ref[k:k + 1, c0:c0 + col_blk]
            cv_ref[t, :, c0:c0 + col_blk] = acc
    for t in range(t_len):
        v = cv_ref[t]
        mu = jnp.mean(v, axis=-1, keepdims=True)
        vc = v - mu
        var = jnp.mean(vc * vc, axis=-1, keepdims=True)
        y = vc * lax.rsqrt(var + EPS) * lng_ref[...] + lnb_ref[...]
        cv_ref[t] = y * _sigmoid(y)
    nb_ref[...] = up_scr[t_len:t_len + nh]


def _conv_tm_call(proj3, buf_t, w_dw, b_dw, ln_g, ln_b, c, col0, bb):
    t_len, bsz, _ = proj3.shape
    nh = CONV_K - 1
    ja, jg = col0 // c, col0 // c + 1
    vec = lambda: pl.BlockSpec((1, c), lambda i: (0, 0))
    return pl.pallas_call(
        functools.partial(_conv_tm_kernel, col_blk=512),
        grid=(bsz // bb,),
        in_specs=[pl.BlockSpec((t_len, bb, c), lambda i: (0, i, ja)),
                  pl.BlockSpec((t_len, bb, c), lambda i: (0, i, jg)),
                  pl.BlockSpec((nh, bb, c), lambda i: (0, i, 0)),
                  pl.BlockSpec((CONV_K, c), lambda i: (0, 0)),
                  vec(), vec(), vec()],
        out_specs=[pl.BlockSpec((t_len, bb, c), lambda i: (0, i, 0)),
                   pl.BlockSpec((nh, bb, c), lambda i: (0, i, 0))],
        out_shape=[jax.ShapeDtypeStruct((t_len, bsz, c), F32),
                   jax.ShapeDtypeStruct((nh, bsz, c), F32)],
        scratch_shapes=[pltpu.VMEM((nh + t_len, bb, c), F32)],
        compiler_params=_cparams("arbitrary"),
        name="conv_tm",
    )(proj3, proj3, buf_t, w_dw, b_dw.reshape(1, c), ln_g.reshape(1, c), ln_b.reshape(1, c))


def _post_mix_kernel(x_ref, o_ref, g1_ref, sh2_ref, sc2_ref, npm_ref, npf_ref, x1_ref, h2_ref):
    def chunk(lead, sub, flat):
        x1 = x_ref[lead, sub, :] + _mod_rows(g1_ref, sub) * _rms(o_ref[flat, :], npm_ref[...])
        x1_ref[lead, sub, :] = x1
        h2 = _rms(x1, npf_ref[...]) * (1.0 + _mod_rows(sc2_ref, sub)) + _mod_rows(sh2_ref, sub)
        h2_ref[flat, :] = h2.astype(BF16)

    _for_row_chunks(x_ref.shape, chunk)


def _post_mix_call(x3, o2d, mod, npm, npf):
    n_rows, d = o2d.shape
    tm = NORM_ROW_TILE
    row = lambda: pl.BlockSpec((tm, d), lambda i: (i, 0))
    vec = lambda: pl.BlockSpec((1, d), lambda i: (0, 0))
    return pl.pallas_call(
        _post_mix_kernel,
        grid=(n_rows // tm,),
        in_specs=[mod.x_spec(tm), row(), mod.spec(2, tm), mod.spec(3, tm), mod.spec(4, tm), vec(), vec()],
        out_specs=[mod.x_spec(tm), row()],
        out_shape=[jax.ShapeDtypeStruct(x3.shape, F32), jax.ShapeDtypeStruct((n_rows, d), BF16)],
        compiler_params=_cparams("arbitrary"),
        name="post_mix",
    )(x3, o2d, mod.arr, mod.arr, mod.arr, npm.reshape(1, d), npf.reshape(1, d))


def _final_kernel(x1_ref, ff_ref, g2_ref, n_ref, o_ref):
    def chunk(lead, sub, flat):
        o_ref[lead, sub, :] = x1_ref[lead, sub, :] + _mod_rows(g2_ref, sub) * _rms(ff_ref[flat, :], n_ref[...])

    _for_row_chunks(x1_ref.shape, chunk)


def _final_call(x1, ff_all, row0, mod, npost):
    n_rows, d = x1.shape[0] * x1.shape[1], x1.shape[2]
    tm = NORM_ROW_TILE
    i0 = row0 // tm
    return pl.pallas_call(
        _final_kernel,
        grid=(n_rows // tm,),
        in_specs=[mod.x_spec(tm),
                  pl.BlockSpec((tm, d), lambda i: (i0 + i, 0)),
                  mod.spec(5, tm),
                  pl.BlockSpec((1, d), lambda i: (0, 0))],
        out_specs=mod.x_spec(tm),
        out_shape=jax.ShapeDtypeStruct(x1.shape, F32),
        compiler_params=_cparams("arbitrary"),
        name="final",
    )(x1, ff_all, mod.arr, npost.reshape(1, d))


def _extract_top(s, n_top):
    rows = s.shape[0]
    rid = lax.broadcasted_iota(jnp.int32, s.shape, 0).astype(F32)
    rank = jnp.full(s.shape, float(rows), F32)
    vals = []
    for it in range(n_top):
        m = jnp.max(s, axis=0, keepdims=True)
        first = jnp.min(jnp.where(s == m, rid, float(rows)), axis=0, keepdims=True)
        sel = rid == first
        rank = jnp.where(sel, float(it), rank)
        s = jnp.where(sel, -jnp.inf, s)
        vals.append(m)
    return vals, rank, jnp.zeros_like(vals[0])


def _extract_top_untied(s, n_top):
    rank = jnp.full(s.shape, float(s.shape[0]), F32)
    vals = []
    for it in range(n_top):
        m = jnp.max(s, axis=0, keepdims=True)
        sel = s == m
        rank = jnp.where(sel, float(it), rank)
        s = jnp.where(sel, -jnp.inf, s)
        vals.append(m)
    taken = jnp.sum(jnp.where(rank < float(n_top), 1.0, 0.0), axis=0, keepdims=True)
    return vals, rank, jnp.abs(taken - float(n_top))


def _route_tables(s0, s1, cand_scr, extract):
    nk = PK_NKEYS
    a_vals, rank0, tied0 = extract(s0, PK_TOPK)
    b_vals, rank1, tied1 = extract(s1, PK_TOPK)
    cand_scr[...] = jnp.full(cand_scr.shape, -jnp.inf, F32)
    for r, (k0, k1) in enumerate(PK_CANDS):
        cand_scr[r:r + 1, :] = a_vals[k0] + b_vals[k1]
    best, rank2, tied2 = extract(cand_scr[...], PK_TOPK)
    z = jnp.zeros_like(best[0])
    for v in best:
        z = z + jnp.exp(v - best[0])
    chosen = jnp.where(rank2 < float(PK_TOPK), 1.0, 0.0)
    cand_row = lax.broadcasted_iota(jnp.int32, (PK_CAND_ROWS, LANES), 0)
    cnt = jnp.zeros((nk, LANES), F32)
    r = 0
    for k0 in range(PK_TOPK):
        n_k0 = sum(1 for cand in PK_CANDS if cand[0] == k0)
        in_k0 = (cand_row >= r) & (cand_row < r + n_k0)
        cnt_k0 = jnp.sum(jnp.where(in_k0, chosen, 0.0), axis=0, keepdims=True)
        cnt = jnp.where(rank0 == float(k0), cnt_k0, cnt)
        r += n_k0
    return (rank1, jnp.exp(s1 - b_vals[0]), cnt, jnp.exp(s0 - a_vals[0]) / z), tied0 + tied1 + tied2


def _route_kernel(h2_ref, wq_ref, keys_ref, r1_ref, e1_ref, cnt_ref, e0_ref, cand_scr):
    nk = PK_NKEYS
    qt = lax.dot_general(wq_ref[...], h2_ref[...], (((1,), (1,)), ((), ())), preferred_element_type=F32)
    s0_all = jnp.dot(keys_ref[0, 0], qt[:nk], precision=HIGHEST, preferred_element_type=F32)
    s1_all = jnp.dot(keys_ref[0, 1], qt[nk:], precision=HIGHEST, preferred_element_type=F32)
    out_refs = (r1_ref, e1_ref, cnt_ref, e0_ref)
    n_chunks = h2_ref.shape[0] // LANES
    tied = []
    for c in range(n_chunks):
        sl = slice(c * LANES, (c + 1) * LANES)
        tables, tied_c = _route_tables(s0_all[:, sl], s1_all[:, sl], cand_scr.at[c], _extract_top_untied)
        tied.append(jnp.max(tied_c))
        for ref, tab in zip(out_refs, tables):
            ref[0, c] = tab
    for c in range(n_chunks):
        sl = slice(c * LANES, (c + 1) * LANES)

        @pl.when(tied[c] > 0.0)
        def _():
            exact, _ = _route_tables(s0_all[:, sl], s1_all[:, sl], cand_scr.at[c], _extract_top)
            for ref, tab in zip(out_refs, exact):
                ref[0, c] = tab


def _route_call(h2_all, wqt_b, keys):
    n_tok, d = h2_all.shape
    heads, _, nk, half = keys.shape
    tt = 512
    out = lambda: pl.BlockSpec((1, tt // LANES, nk, LANES), lambda i, h: (h, i, 0, 0))
    return pl.pallas_call(
        _route_kernel,
        grid=(n_tok // tt, heads),
        in_specs=[pl.BlockSpec((tt, d), lambda i, h: (i, 0)),
                  pl.BlockSpec((2 * half, d), lambda i, h: (h, 0)),
                  pl.BlockSpec((1, 2, nk, half), lambda i, h: (h, 0, 0, 0))],
        out_specs=[out(), out(), out(), out()],
        out_shape=[jax.ShapeDtypeStruct((heads, n_tok // LANES, nk, LANES), F32)] * 4,
        scratch_shapes=[pltpu.VMEM((tt // LANES, PK_CAND_ROWS, LANES), F32)],
        compiler_params=_cparams("arbitrary", "arbitrary"),
        name="peer_route",
    )(h2_all, wqt_b, keys)


def _experts_kernel(h2_ref, u_ref, v_ref, r1_ref, e1_ref, cnt_ref, e0_ref, o_ref, act_scr, w_scr):
    eb, tt = act_scr.shape
    nk = PK_NKEYS
    e = pl.program_id(1)

    @pl.when(e == 0)
    def _():
        o_ref[...] = jnp.zeros_like(o_ref)

    st = lax.dot_general(u_ref[...], h2_ref[...], (((1,), (1,)), ((), ())), preferred_element_type=F32)
    act_scr[...] = _gelu_tanh(st)
    for ii in range(eb // nk):
        row = pl.ds(e * (eb // nk) + ii, 1)
        for c in range(tt // LANES):
            sl = slice(c * LANES, (c + 1) * LANES)
            w = jnp.zeros((nk, LANES), F32)
            for h in range(PK_HEADS):
                w = w + jnp.where(r1_ref[h, c] < cnt_ref[h, c, row, :], e1_ref[h, c] * e0_ref[h, c, row, :], 0.0)
            w_scr[ii * nk:(ii + 1) * nk, sl] = (w * act_scr[ii * nk:(ii + 1) * nk, sl]).astype(BF16)
    o_ref[...] += lax.dot_general(w_scr[...], v_ref[...], (((0,), (0,)), ((), ())),
                                  preferred_element_type=F32)


def _experts_call(h2_all, u_b, v_b, r1, e1, cnt, e0):
    n_tok, d = h2_all.shape
    n_exp = u_b.shape[0]
    heads, _, nk, _ = r1.shape
    tt, eb = 512, 512
    once = pl.Buffered(1)
    aux = lambda: pl.BlockSpec((heads, tt // LANES, nk, LANES), lambda i, e: (0, i, 0, 0), pipeline_mode=once)
    return pl.pallas_call(
        _experts_kernel,
        grid=(n_tok // tt, n_exp // eb),
        in_specs=[pl.BlockSpec((tt, d), lambda i, e: (i, 0), pipeline_mode=once),
                  pl.BlockSpec((eb, d), lambda i, e: (e, 0)),
                  pl.BlockSpec((eb, d), lambda i, e: (e, 0)),
                  aux(), aux(), aux(), aux()],
        out_specs=pl.BlockSpec((tt, d), lambda i, e: (i, 0)),
        out_shape=jax.ShapeDtypeStruct((n_tok, d), F32),
        scratch_shapes=[pltpu.VMEM((eb, tt), F32), pltpu.VMEM((eb, tt), BF16)],
        compiler_params=_cparams("arbitrary", "arbitrary"),
        name="peer_experts",
    )(h2_all, u_b, v_b, r1, e1, cnt, e0)


def _mixer(x3, mod, h0_re, h0_im, conv_buf, w):
    d = x3.shape[2]
    n_rows = x3.shape[0] * x3.shape[1]
    proj = _inproj_call(x3, mod, w["norm_pre_mix"], w["w_in_b"])
    proj3 = proj.reshape(x3.shape[0], x3.shape[1], -1)
    sw = w["ssm_b_glu"].shape[0]
    cw = w["conv_b_dw"].shape[0]

    t_len, bsz = (x3.shape[0], x3.shape[1]) if mod.time_major else (x3.shape[1], x3.shape[0])
    y_ssm, new_re, new_im = _ssm_branch(proj, bsz, t_len, mod.time_major, h0_re, h0_im, w["ssm"])
    z = _glu_call(y_ssm, w["w_glu_b"], w["ssm_b_glu"])
    y_a = _gated_proj_call(z, w["w_proj_b"], proj, sw + 2 * cw)

    conv_w = (w["conv_w_dw"], w["conv_b_dw"], w["conv_ln_g"], w["conv_ln_b"], cw, sw)
    if mod.time_major:
        cv, nb_t = _conv_tm_call(proj3, jnp.transpose(conv_buf, (1, 0, 2)), *conv_w, 16)
        new_buf = jnp.transpose(nb_t, (1, 0, 2))
    else:
        if conv_buf is None:
            conv_buf = jnp.zeros((x3.shape[0], CONV_K - 1, cw), F32)
        cv, new_buf = _conv_call(proj3, conv_buf, *conv_w, 1, 256)
    mix = _gated_proj_call(cv.reshape(n_rows, cw), w["w_pw_b"], proj, sw + 2 * cw + d, add=y_a, out_dtype=BF16)
    o = _matmul_call(mix, w["w_out_b"])
    x1, h2 = _post_mix_call(x3, o, mod, w["norm_post_mix"], w["norm_pre_ffn"])
    return x1, h2, new_re, new_im, new_buf


def _layer(xp, xs, c_prompt, c_sample, st_re, st_im, st_conv, w):
    bp, tp, d = xp.shape
    bs, ts, _ = xs.shape

    pad = (-(bp + bs)) % 8
    c_all = jnp.concatenate([c_prompt, c_sample, jnp.zeros((pad, d), F32)], axis=0)
    mod_all = _mod_call(c_all, w["w_mod"], w["b_mod"])
    mod_p = _Mod(mod_all[:bp].reshape(bp, 1, N_MOD * d), False, tp, d)
    mod_s = _Mod(mod_all[bp:bp + bs], True, bs, d)

    x1p, h2p, p_re, p_im, p_buf = _mixer(xp, mod_p, None, None, None, w)
    x1s, h2s, s_re, s_im, s_buf = _mixer(jnp.transpose(xs, (1, 0, 2)), mod_s, st_re, st_im, st_conv, w)

    h2_all = jnp.concatenate([h2p, h2s], axis=0)
    r1, e1, cnt, e0 = _route_call(h2_all, w["w_qt_b"], w["peer_sub_keys"])
    ff = _experts_call(h2_all, w["peer_u_b"], w["peer_v_b"], r1, e1, cnt, e0)

    yp = _final_call(x1p, ff, 0, mod_p, w["norm_post_ffn"])
    ys = jnp.transpose(_final_call(x1s, ff, bp * tp, mod_s, w["norm_post_ffn"]), (1, 0, 2))
    return yp, ys, p_re, p_im, p_buf, s_re, s_im, s_buf


def kernel(x_prompt, x_sample, c_prompt, c_sample, state_ssm_re, state_ssm_im, state_conv, w_mod, b_mod,
           norm_pre_mix, norm_post_mix, norm_pre_ffn, norm_post_ffn, w_in, ssm_lambda_re, ssm_lambda_im,
           ssm_log_dt, ssm_b_re, ssm_b_im, ssm_c_re, ssm_c_im, ssm_d, ssm_w_glu, ssm_b_glu, w_proj_ssm,
           conv_w_dw, conv_b_dw, conv_ln_g, conv_ln_b, conv_w_pw, w_out, peer_w_q, peer_sub_keys, peer_u,
           peer_v):
    depth = w_mod.shape[0]
    xp, xs = x_prompt, x_sample
    outs = [[] for _ in range(6)]
    for l in range(depth):
        w = {
            "w_mod": w_mod[l], "b_mod": b_mod[l],
            "norm_pre_mix": norm_pre_mix[l], "norm_post_mix": norm_post_mix[l],
            "norm_pre_ffn": norm_pre_ffn[l], "norm_post_ffn": norm_post_ffn[l],
            "w_in_b": w_in[l].astype(BF16),
            "ssm": (ssm_lambda_re[l], ssm_lambda_im[l], ssm_log_dt[l], ssm_b_re[l], ssm_b_im[l],
                    ssm_c_re[l], ssm_c_im[l], ssm_d[l]),
            "w_glu_b": ssm_w_glu[l].astype(BF16), "ssm_b_glu": ssm_b_glu[l],
            "w_proj_b": w_proj_ssm[l].astype(BF16),
            "conv_w_dw": conv_w_dw[l], "conv_b_dw": conv_b_dw[l],
            "conv_ln_g": conv_ln_g[l], "conv_ln_b": conv_ln_b[l],
            "w_pw_b": conv_w_pw[l].astype(BF16), "w_out_b": w_out[l].astype(BF16),
            "w_qt_b": jnp.transpose(peer_w_q[l]).astype(BF16),
            "peer_sub_keys": peer_sub_keys[l],
            "peer_u_b": peer_u[l].astype(BF16), "peer_v_b": peer_v[l].astype(BF16),
        }
        xp, xs, *states = _layer(xp, xs, c_prompt, c_sample, state_ssm_re[l], state_ssm_im[l], state_conv[l], w)
        for acc, s in zip(outs, states):
            acc.append(s)
    return (xp, xs) + tuple(jnp.stack(o) for o in outs)
```

```python
import functools
import math

import jax
import jax.numpy as jnp
from jax import lax
from jax.experimental import pallas as pl
from jax.experimental.pallas import tpu as pltpu

F32 = jnp.float32
BF16 = jnp.bfloat16
HIGHEST = lax.Precision.HIGHEST

EPS = 1e-6
N_MOD = 6
SSM_GROUP = 16
SSM_STATE = 64
CONV_K = 31
CONV_HIST = 32
PK_HEADS = 8
PK_NKEYS = 128
PK_TOPK = 16
SSM_CHUNK = 16

V7X_VMEM_BYTES = 64 * 1024 * 1024
VMEM_CAP_MB = 56
LANES = 128
SUBLANES = 8
ROW_TILE = 512
NORM_ROW_TILE = 256
COL_TILE = 1024

PK_CANDS = tuple((k0, k1) for k0 in range(PK_TOPK) for k1 in range(PK_TOPK)
                 if (k0 + 1) * (k1 + 1) <= PK_TOPK)
PK_CAND_ROWS = 56


def _cparams(*sem):
    assert VMEM_CAP_MB * 1024 * 1024 <= V7X_VMEM_BYTES
    return pltpu.CompilerParams(dimension_semantics=sem, vmem_limit_bytes=VMEM_CAP_MB * 1024 * 1024)


def _sigmoid(x):
    return 1.0 / (1.0 + jnp.exp(-x))


def _gelu_tanh(x):
    c = math.sqrt(2.0 / math.pi)
    return 0.5 * x * (1.0 + jnp.tanh(c * (x + 0.044715 * (x * x * x))))


def _rms(x, g):
    return x * lax.rsqrt(jnp.mean(x * x, axis=-1, keepdims=True) + EPS) * g


def _mod_kernel(c_ref, w_ref, b_ref, o_ref):
    c = c_ref[...]
    a = (c * _sigmoid(c)).astype(BF16)
    o_ref[...] = jnp.dot(a, w_ref[...].astype(BF16), preferred_element_type=F32) + b_ref[...]


def _mod_call(c_all, w_mod, b_mod):
    m, d = c_all.shape
    n = w_mod.shape[1]
    tn = 512
    return pl.pallas_call(
        _mod_kernel,
        grid=(n // tn,),
        in_specs=[pl.BlockSpec((m, d), lambda j: (0, 0)),
                  pl.BlockSpec((d, tn), lambda j: (0, j)),
                  pl.BlockSpec((1, tn), lambda j: (0, j))],
        out_specs=pl.BlockSpec((m, tn), lambda j: (0, j)),
        out_shape=jax.ShapeDtypeStruct((m, n), F32),
        compiler_params=_cparams("arbitrary"),
        name="mod",
    )(c_all, w_mod, b_mod.reshape(1, n))


class _Mod:
    def __init__(self, arr, time_major, seq, d):
        self.arr, self.time_major, self.seq, self.d = arr, time_major, seq, d

    def x_spec(self, rows):
        if self.time_major:
            assert rows % self.seq == 0
            return pl.BlockSpec((rows // self.seq, self.seq, self.d), lambda i, *_: (i, 0, 0))
        assert self.seq % rows == 0
        tpb = self.seq // rows
        return pl.BlockSpec((1, rows, self.d), lambda i, *_: (i // tpb, i % tpb, 0))

    def spec(self, k, rows):
        if self.time_major:
            return pl.BlockSpec((self.seq, self.d), lambda i, *_: (0, k))
        tpb = self.seq // rows
        return pl.BlockSpec((None, 1, self.d), lambda i, *_: (i // tpb, 0, k))


NORM_CHUNK_ROWS = 64


def _for_row_chunks(tile_shape, fn):
    n_lead, n_rows, _ = tile_shape
    cr = NORM_CHUNK_ROWS
    per_lead = n_rows // cr

    def body(c, carry):
        sub = pl.ds(pl.multiple_of((c % per_lead) * cr, cr), cr)
        fn(c // per_lead, sub, pl.ds(pl.multiple_of(c * cr, cr), cr))
        return carry

    lax.fori_loop(0, n_lead * per_lead, body, 0)


def _mod_rows(ref, sub):
    return ref[...] if ref.shape[0] == 1 else ref[sub, :]


def _inproj_kernel(x_ref, sh_ref, sc_ref, g_ref, w_ref, o_ref, h_scr):
    @pl.when(pl.program_id(1) == 0)
    def _():
        def chunk(lead, sub, flat):
            h = _rms(x_ref[lead, sub, :], g_ref[...]) * (1.0 + _mod_rows(sc_ref, sub)) + _mod_rows(sh_ref, sub)
            h_scr[flat, :] = h.astype(BF16)

        _for_row_chunks(x_ref.shape, chunk)

    o_ref[...] = jnp.dot(h_scr[...], w_ref[...], preferred_element_type=F32)


def _inproj_call(x3, mod, g, w_b):
    n_rows, d = x3.shape[0] * x3.shape[1], x3.shape[2]
    n = w_b.shape[1]
    tm, tn = ROW_TILE, COL_TILE
    return pl.pallas_call(
        _inproj_kernel,
        grid=(n_rows // tm, n // tn),
        in_specs=[mod.x_spec(tm), mod.spec(0, tm), mod.spec(1, tm),
                  pl.BlockSpec((1, d), lambda i, j: (0, 0)),
                  pl.BlockSpec((d, tn), lambda i, j: (0, j))],
        out_specs=pl.BlockSpec((tm, tn), lambda i, j: (i, j)),
        out_shape=jax.ShapeDtypeStruct((n_rows, n), F32),
        scratch_shapes=[pltpu.VMEM((tm, d), BF16)],
        compiler_params=_cparams("arbitrary", "arbitrary"),
        name="inproj",
    )(x3, mod.arr, mod.arr, g.reshape(1, d), w_b)


def _glu_kernel(y_ref, yt_ref, w_ref, b_ref, o_ref, a_scr):
    @pl.when(pl.program_id(1) == 0)
    def _():
        a_scr[...] = _gelu_tanh(y_ref[...]).astype(BF16)

    acc = jnp.dot(a_scr[...], w_ref[...], preferred_element_type=F32) + b_ref[...]
    o_ref[...] = (_gelu_tanh(yt_ref[...]) * _sigmoid(acc)).astype(BF16)


def _glu_call(y2d, w_b, b):
    n_rows, k = y2d.shape
    n = w_b.shape[1]
    tm, tn = ROW_TILE, COL_TILE
    return pl.pallas_call(
        _glu_kernel,
        grid=(n_rows // tm, n // tn),
        in_specs=[pl.BlockSpec((tm, k), lambda i, j: (i, 0)),
                  pl.BlockSpec((tm, tn), lambda i, j: (i, j)),
                  pl.BlockSpec((k, tn), lambda i, j: (0, j)),
                  pl.BlockSpec((1, tn), lambda i, j: (0, j))],
        out_specs=pl.BlockSpec((tm, tn), lambda i, j: (i, j)),
        out_shape=jax.ShapeDtypeStruct((n_rows, n), BF16),
        scratch_shapes=[pltpu.VMEM((tm, k), BF16)],
        compiler_params=_cparams("arbitrary", "arbitrary"),
        name="ssm_glu",
    )(y2d, y2d, w_b, b.reshape(1, n))


def _gated_proj_kernel(z_ref, w_ref, gate_ref, o_ref):
    acc = jnp.dot(z_ref[...].astype(BF16), w_ref[...], preferred_element_type=F32)
    o_ref[...] = _sigmoid(gate_ref[...]) * acc


def _gated_add_proj_kernel(z_ref, w_ref, gate_ref, a_ref, o_ref):
    acc = jnp.dot(z_ref[...].astype(BF16), w_ref[...], preferred_element_type=F32)
    o_ref[...] = (a_ref[...] + _sigmoid(gate_ref[...]) * acc).astype(o_ref.dtype)


def _gated_proj_call(z2d, w_b, proj, gate_col0, add=None, out_dtype=F32):
    n_rows, k = z2d.shape
    n = w_b.shape[1]
    tm, tn = ROW_TILE, COL_TILE
    gj = gate_col0 // tn
    in_specs = [pl.BlockSpec((tm, k), lambda i, j: (i, 0)),
                pl.BlockSpec((k, tn), lambda i, j: (0, j)),
                pl.BlockSpec((tm, tn), lambda i, j: (i, gj + j))]
    args = [z2d, w_b, proj]
    if add is not None:
        in_specs.append(pl.BlockSpec((tm, tn), lambda i, j: (i, j)))
        args.append(add)
    return pl.pallas_call(
        _gated_proj_kernel if add is None else _gated_add_proj_kernel,
        grid=(n_rows // tm, n // tn),
        in_specs=in_specs,
        out_specs=pl.BlockSpec((tm, tn), lambda i, j: (i, j)),
        out_shape=jax.ShapeDtypeStruct((n_rows, n), out_dtype),
        compiler_params=_cparams("arbitrary", "arbitrary"),
        name="gated_proj" if add is None else "gated_add_proj",
    )(*args)


def _matmul_kernel(a_ref, w_ref, o_ref):
    o_ref[...] = jnp.dot(a_ref[...], w_ref[...], preferred_element_type=F32)


def _matmul_call(a2d, w_b):
    n_rows, k = a2d.shape
    n = w_b.shape[1]
    tm, tn = ROW_TILE, COL_TILE
    return pl.pallas_call(
        _matmul_kernel,
        grid=(n_rows // tm, n // tn),
        in_specs=[pl.BlockSpec((tm, k), lambda i, j: (i, 0)),
                  pl.BlockSpec((k, tn), lambda i, j: (0, j))],
        out_specs=pl.BlockSpec((tm, tn), lambda i, j: (i, j)),
        out_shape=jax.ShapeDtypeStruct((n_rows, n), F32),
        compiler_params=_cparams("arbitrary", "arbitrary"),
        name="out_proj",
    )(a2d, w_b)


def _ssm_param_kernel(lrc_ref, lic_ref, lrr_ref, lir_ref, ldt_ref, ctre_ref, ctim_ref, btre_ref, btim_ref,
                      dpad_ref, t_ref, sre_ref, sim_ref, rre_ref, rim_ref, lre_ref, lim_ref, *, chunk):
    p, n = SSM_GROUP, SSM_STATE
    lp = chunk * p
    sre_ref[...] = jnp.zeros_like(sre_ref)
    sim_ref[...] = jnp.zeros_like(sim_ref)
    rre_ref[...] = jnp.zeros_like(rre_ref)
    rim_ref[...] = jnp.zeros_like(rim_ref)
    lane = lax.broadcasted_iota(jnp.int32, (p, lp), 1)
    diag = lax.broadcasted_iota(jnp.int32, (p, lp), 0) == lane
    k_lane = (lax.broadcasted_iota(jnp.int32, (n, lp), 1) // p).astype(F32)
    e_row = (chunk - 1 - lax.broadcasted_iota(jnp.int32, (chunk, n), 0)).astype(F32)
    for a in range(2):
        dt = jnp.exp(ldt_ref[a])
        lrc, lic = lrc_ref[a] * dt, lic_ref[a] * dt
        lrr, lir = lrr_ref[a], lir_ref[a]
        m0, m1 = jnp.exp(lrc * k_lane), jnp.exp(lrc * (k_lane + 1.0))
        p0re, p0im = m0 * jnp.cos(lic * k_lane), m0 * jnp.sin(lic * k_lane)
        p1re, p1im = m1 * jnp.cos(lic * (k_lane + 1.0)), m1 * jnp.sin(lic * (k_lane + 1.0))
        ctre, ctim = ctre_ref[a], ctim_ref[a]
        dec = jnp.exp(lrr * dt)
        abre, abim = dec * jnp.cos(lir * dt), dec * jnp.sin(lir * dt)
        den = lrr * lrr + lir * lir
        fre = ((abre - 1.0) * lrr + abim * lir) / den
        fim = (abim * lrr - (abre - 1.0) * lir) / den
        bre = fre * btre_ref[a] - fim * btim_ref[a]
        bim = fre * btim_ref[a] + fim * btre_ref[a]
        clre = ctre * p0re - ctim * p0im
        clim = ctre * p0im + ctim * p0re
        kt = (jnp.dot(bre, clre, precision=HIGHEST, preferred_element_type=F32)
              - jnp.dot(bim, clim, precision=HIGHEST, preferred_element_type=F32))
        kt = kt + jnp.where(diag, dpad_ref[a], 0.0)
        for s in range(chunk):
            blk = kt if s == 0 else jnp.where(lane >= s * p, pltpu.roll(kt, s * p, 1), 0.0)
            t_ref[a, s * p:(s + 1) * p, :] = blk.astype(BF16)
        mr = jnp.exp(lrr * dt * e_row)
        prre, prim = mr * jnp.cos(lir * dt * e_row), mr * jnp.sin(lir * dt * e_row)
        for s in range(chunk):
            pr, pi = prre[s:s + 1, :], prim[s:s + 1, :]
            r0 = a * lp + s * p
            sre_ref[0, r0:r0 + p, a * n:(a + 1) * n] = bre * pr - bim * pi
            sim_ref[0, r0:r0 + p, a * n:(a + 1) * n] = bre * pi + bim * pr
        rre_ref[0, a * n:(a + 1) * n, a * lp:(a + 1) * lp] = (ctre * p1re - ctim * p1im).astype(BF16)
        rim_ref[0, a * n:(a + 1) * n, a * lp:(a + 1) * lp] = (-(ctre * p1im + ctim * p1re)).astype(BF16)
        ml = jnp.exp(lrr * dt * float(chunk))
        lre_ref[0, :, a * n:(a + 1) * n] = ml * jnp.cos(lir * dt * float(chunk))
        lim_ref[0, :, a * n:(a + 1) * n] = ml * jnp.sin(lir * dt * float(chunk))


def _ssm_param_call(lam_re, lam_im, log_dt, b_re, b_im, c_re, c_im, d_skip, chunk):
    g, n = lam_re.shape
    p = SSM_GROUP
    lp = chunk * p
    ct_re = jnp.tile(jnp.transpose(c_re, (0, 2, 1)), (1, 1, chunk))
    ct_im = jnp.tile(jnp.transpose(c_im, (0, 2, 1)), (1, 1, chunk))
    bt_re = jnp.transpose(b_re, (0, 2, 1))
    bt_im = jnp.transpose(b_im, (0, 2, 1))
    dpad = jnp.pad(d_skip.reshape(g, 1, p), ((0, 0), (0, 0), (0, lp - p)))
    spec3 = lambda s: pl.BlockSpec((2,) + s, lambda i: (i, 0, 0))
    pair = lambda s: pl.BlockSpec((1,) + s, lambda i: (i, 0, 0))
    return pl.pallas_call(
        functools.partial(_ssm_param_kernel, chunk=chunk),
        grid=(g // 2,),
        in_specs=[spec3((n, 1)), spec3((n, 1)), spec3((1, n)), spec3((1, n)), spec3((1, 1)),
                  spec3((n, lp)), spec3((n, lp)), spec3((p, n)), spec3((p, n)), spec3((1, lp))],
        out_specs=[spec3((lp, lp)), pair((2 * lp, 2 * n)), pair((2 * lp, 2 * n)),
                   pair((2 * n, 2 * lp)), pair((2 * n, 2 * lp)), pair((1, 2 * n)), pair((1, 2 * n))],
        out_shape=[jax.ShapeDtypeStruct((g, lp, lp), BF16),
                   jax.ShapeDtypeStruct((g // 2, 2 * lp, 2 * n), F32),
                   jax.ShapeDtypeStruct((g // 2, 2 * lp, 2 * n), F32),
                   jax.ShapeDtypeStruct((g // 2, 2 * n, 2 * lp), BF16),
                   jax.ShapeDtypeStruct((g // 2, 2 * n, 2 * lp), BF16),
                   jax.ShapeDtypeStruct((g // 2, 1, 2 * n), F32),
                   jax.ShapeDtypeStruct((g // 2, 1, 2 * n), F32)],
        compiler_params=_cparams("arbitrary"),
        name="ssm_params",
    )(lam_re.reshape(g, n, 1), lam_im.reshape(g, n, 1), lam_re.reshape(g, 1, n), lam_im.reshape(g, 1, n),
      log_dt.reshape(g, 1, 1), ct_re, ct_im, bt_re, bt_im, dpad)


SSM_LANE_GROUPS = LANES // SSM_GROUP
SSM_ROW_BLK = 16


def _transpose_lane_blocks(load_src, store_dst):
    n = SSM_LANE_GROUPS
    tiles = [load_src(i) for i in range(n)]
    lane_blk = lax.broadcasted_iota(jnp.int32, tiles[0].shape, 1) // SSM_GROUP
    d = 1
    while d < n:
        upper = (lane_blk & d) != 0
        nxt = list(tiles)
        for a in range(n):
            if a & d == 0:
                ta, tb = tiles[a], tiles[a + d]
                nxt[a] = jnp.where(upper, pltpu.roll(tb, d * SSM_GROUP, 1), ta)
                nxt[a + d] = jnp.where(upper, tb, pltpu.roll(ta, LANES - d * SSM_GROUP, 1))
        tiles = nxt
        d *= 2
    for dst in range(n):
        store_dst(dst, tiles[dst])


def _ssm_main_kernel(u_ref, t_ref, sre_ref, sim_ref, rre_ref, rim_ref, lre_ref, lim_ref, h0re_ref, h0im_ref,
                     y_ref, hnre_ref, hnim_ref, x_scr, yg_scr, vre_scr, vim_scr, hre_scr, him_scr,
                     *, batch, chunks, chunk, time_major):
    rows = batch * chunks
    lp = chunk * SSM_GROUP
    rb = SSM_ROW_BLK

    def token_rows(r0, step):
        if time_major:
            return pl.ds(step * batch + r0, rb)
        return pl.ds(r0 * chunk + step, rb, stride=chunk)

    def regroup(i, carry):
        r0 = pl.multiple_of(i * rb, rb)
        for cb in range(lp // LANES):
            lanes = slice(cb * LANES, (cb + 1) * LANES)
            steps = cb * SSM_LANE_GROUPS

            def store_x(g, val):
                x_scr[g, pl.ds(r0, rb), lanes] = val

            _transpose_lane_blocks(lambda s8: u_ref[token_rows(r0, steps + s8), :], store_x)
        return carry

    lax.fori_loop(0, rows // rb, regroup, 0, unroll=2)

    def pair(pr, carry):
        u0, u1 = x_scr[2 * pr], x_scr[2 * pr + 1]
        ucat = jnp.concatenate([u0, u1], axis=1)
        vre_scr[...] = jnp.dot(ucat, sre_ref[pr], precision=HIGHEST, preferred_element_type=F32)
        vim_scr[...] = jnp.dot(ucat, sim_ref[pr], precision=HIGHEST, preferred_element_type=F32)
        are, aim = lre_ref[pr], lim_ref[pr]
        if chunks == 1:
            hre, him = h0re_ref[pr], h0im_ref[pr]
            hre_scr[...] = hre
            him_scr[...] = him
            hnre_ref[pr] = are * hre - aim * him + vre_scr[...]
            hnim_ref[pr] = are * him + aim * hre + vim_scr[...]
        else:
            init = (tuple(h0re_ref[pr, b:b + 1, :] for b in range(batch))
                    + tuple(h0im_ref[pr, b:b + 1, :] for b in range(batch)))

            def body(m, hs):
                nre, nim = [], []
                for b in range(batch):
                    hr, hi = hs[b], hs[batch + b]
                    row = b * chunks + m
                    hre_scr[pl.ds(row, 1), :] = hr
                    him_scr[pl.ds(row, 1), :] = hi
                    nre.append(are * hr - aim * hi + vre_scr[pl.ds(row, 1), :])
                    nim.append(are * hi + aim * hr + vim_scr[pl.ds(row, 1), :])
                return tuple(nre) + tuple(nim)

            fin = lax.fori_loop(0, chunks, body, init)
            for b in range(batch):
                hnre_ref[pr, b:b + 1, :] = fin[b]
                hnim_ref[pr, b:b + 1, :] = fin[batch + b]
        yh = (jnp.dot(hre_scr[...].astype(BF16), rre_ref[pr], preferred_element_type=F32)
              + jnp.dot(him_scr[...].astype(BF16), rim_ref[pr], preferred_element_type=F32))
        yg_scr[2 * pr] = jnp.dot(u0.astype(BF16), t_ref[2 * pr], preferred_element_type=F32) + yh[:, :lp]
        yg_scr[2 * pr + 1] = jnp.dot(u1.astype(BF16), t_ref[2 * pr + 1], preferred_element_type=F32) + yh[:, lp:]
        return carry

    lax.fori_loop(0, SSM_LANE_GROUPS // 2, pair, 0)

    def ungroup(i, carry):
        r0 = pl.multiple_of(i * rb, rb)
        for cb in range(lp // LANES):
            lanes = slice(cb * LANES, (cb + 1) * LANES)
            steps = cb * SSM_LANE_GROUPS

            def store_y(t8, val):
                y_ref[token_rows(r0, steps + t8), :] = val

            _transpose_lane_blocks(lambda g: yg_scr[g, pl.ds(r0, rb), lanes], store_y)
        return carry

    lax.fori_loop(0, rows // rb, ungroup, 0, unroll=2)


def _ssm_main_call(proj, ops, h0re, h0im, batch, chunks, chunk, time_major):
    t_op, s_re, s_im, r_re, r_im, l_re, l_im = ops
    n_tok = proj.shape[0]
    g, lp, _ = t_op.shape
    rows = batch * chunks
    assert rows % SSM_ROW_BLK == 0
    n2 = 2 * SSM_STATE
    ng = SSM_LANE_GROUPS
    grp = lambda s: pl.BlockSpec((ng,) + s, lambda j: (j, 0, 0))
    pair = lambda s: pl.BlockSpec((ng // 2,) + s, lambda j: (j, 0, 0))
    tok = lambda: pl.BlockSpec((n_tok, LANES), lambda j: (0, j))
    return pl.pallas_call(
        functools.partial(_ssm_main_kernel, batch=batch, chunks=chunks, chunk=chunk, time_major=time_major),
        grid=(g // ng,),
        in_specs=[tok(), grp((lp, lp)), pair((2 * lp, n2)), pair((2 * lp, n2)),
                  pair((n2, 2 * lp)), pair((n2, 2 * lp)), pair((1, n2)), pair((1, n2)),
                  pair((batch, n2)), pair((batch, n2))],
        out_specs=[tok(), pair((batch, n2)), pair((batch, n2))],
        out_shape=[jax.ShapeDtypeStruct((n_tok, g * SSM_GROUP), F32),
                   jax.ShapeDtypeStruct((g // 2, batch, n2), F32),
                   jax.ShapeDtypeStruct((g // 2, batch, n2), F32)],
        scratch_shapes=[pltpu.VMEM((ng, rows, lp), F32)] * 2 + [pltpu.VMEM((rows, n2), F32)] * 4,
        compiler_params=_cparams("arbitrary"),
        name="ssm_main",
    )(proj, t_op, s_re, s_im, r_re, r_im, l_re, l_im, h0re, h0im)


def _ssm_branch(proj, bsz, t_len, time_major, h0_re, h0_im, ssm_w):
    lam_re, lam_im, log_dt, b_re, b_im, c_re, c_im, d_skip = ssm_w
    g, n = lam_re.shape
    chunk = min(SSM_CHUNK, t_len)
    assert t_len % chunk == 0 and (chunk * SSM_GROUP) % LANES == 0
    chunks = t_len // chunk
    assert not time_major or chunks == 1
    ops = _ssm_param_call(lam_re, lam_im, log_dt, b_re, b_im, c_re, c_im, d_skip, chunk)
    if h0_re is None:
        h0re = jnp.zeros((g // 2, bsz, 2 * n), F32)
        h0im = h0re
    else:
        h0re = jnp.transpose(h0_re.reshape(bsz, g // 2, 2 * n), (1, 0, 2))
        h0im = jnp.transpose(h0_im.reshape(bsz, g // 2, 2 * n), (1, 0, 2))
    y, hn_re, hn_im = _ssm_main_call(proj, ops, h0re, h0im, bsz, chunks, chunk, time_major)
    new_re = jnp.transpose(hn_re, (1, 0, 2)).reshape(bsz, g, n)
    new_im = jnp.transpose(hn_im, (1, 0, 2)).reshape(bsz, g, n)
    return y, new_re, new_im


def _conv_kernel(a_ref, g_ref, buf_ref, w_ref, bdw_ref, lng_ref, lnb_ref, cv_ref, nb_ref, up_scr, sh_scr, acc_scr,
                 *, row_blk, col_blk):
    bb, tt, c = a_ref.shape
    hist = CONV_HIST
    off = hist - (CONV_K - 1)
    j = pl.program_id(1)

    @pl.when(j == 0)
    def _():
        up_scr[:, off:hist, :] = buf_ref[...]

    up_scr[:, hist:hist + tt, :] = a_ref[...] * _sigmoid(g_ref[...])
    n_sh = sh_scr.shape[2]
    for r in range(1, SUBLANES):
        sh_scr[r - 1] = up_scr[:, r:r + n_sh, :]
    for r0 in range(0, tt, row_blk):
        for c0 in range(0, c, col_blk):
            acc = jnp.zeros((bb, row_blk, col_blk), F32) + bdw_ref[:, c0:c0 + col_blk][None]
            for k in range(CONV_K):
                r = (k + off) % SUBLANES
                base = r0 + k + off - r
                if r == 0:
                    tap = up_scr[:, base:base + row_blk, c0:c0 + col_blk]
                else:
                    tap = sh_scr[r - 1, :, base:base + row_blk, c0:c0 + col_blk]
                acc = acc + tap * w_ref[k:k + 1, c0:c0 + col_blk][None]
            acc_scr[:, r0:r0 + row_blk, c0:c0 + col_blk] = acc
    v = acc_scr[...]
    mu = jnp.mean(v, axis=-1, keepdims=True)
    vc = v - mu
    var = jnp.mean(vc * vc, axis=-1, keepdims=True)
    y = vc * lax.rsqrt(var + EPS) * lng_ref[...][None] + lnb_ref[...][None]
    cv_ref[...] = y * _sigmoid(y)
    tail = up_scr[:, tt + off:tt + hist, :]

    @pl.when(j == pl.num_programs(1) - 1)
    def _():
        nb_ref[...] = tail

    up_scr[:, off:hist, :] = tail


def _conv_call(proj3, buf, w_dw, b_dw, ln_g, ln_b, c, col0, bb, tt):
    bsz, t_len, _ = proj3.shape
    ja, jg = col0 // c, col0 // c + 1
    row_blk = min(tt, 64)
    vec = lambda: pl.BlockSpec((1, c), lambda i, j: (0, 0))
    return pl.pallas_call(
        functools.partial(_conv_kernel, row_blk=row_blk, col_blk=256),
        grid=(bsz // bb, t_len // tt),
        in_specs=[pl.BlockSpec((bb, tt, c), lambda i, j: (i, j, ja)),
                  pl.BlockSpec((bb, tt, c), lambda i, j: (i, j, jg)),
                  pl.BlockSpec((bb, CONV_K - 1, c), lambda i, j: (i, 0, 0)),
                  pl.BlockSpec((CONV_K, c), lambda i, j: (0, 0)),
                  vec(), vec(), vec()],
        out_specs=[pl.BlockSpec((bb, tt, c), lambda i, j: (i, j, 0)),
                   pl.BlockSpec((bb, CONV_K - 1, c), lambda i, j: (i, 0, 0))],
        out_shape=[jax.ShapeDtypeStruct((bsz, t_len, c), F32),
                   jax.ShapeDtypeStruct((bsz, CONV_K - 1, c), F32)],
        scratch_shapes=[pltpu.VMEM((bb, CONV_HIST + tt, c), F32),
                        pltpu.VMEM((SUBLANES - 1, bb, CONV_HIST + tt - SUBLANES, c), F32),
                        pltpu.VMEM((bb, tt, c), F32)],
        compiler_params=_cparams("arbitrary", "arbitrary"),
        name="conv",
    )(proj3, proj3, buf, w_dw, b_dw.reshape(1, c), ln_g.reshape(1, c), ln_b.reshape(1, c))


def _conv_tm_kernel(a_ref, g_ref, buf_ref, w_ref, bdw_ref, lng_ref, lnb_ref, cv_ref, nb_ref, up_scr, *, col_blk):
    t_len, bb, c = a_ref.shape
    nh = CONV_K - 1
    up_scr[0:nh] = buf_ref[...]
    up_scr[nh:nh + t_len] = a_ref[...] * _sigmoid(g_ref[...])
    for t in range(t_len):
        for c0 in range(0, c, col_blk):
            acc = jnp.zeros((bb, col_blk), F32) + bdw_ref[:, c0:c0 + col_blk]
            for k in range(CONV_K):
                acc = acc + up_scr[t + k, :, c0:c0 + col_blk] * w_ref[k:k + 1, c0:c0 + col_blk]
            cv_ref[t, :, c0:c0 + col_blk] = acc
    for t in range(t_len):
        v = cv_ref[t]
        mu = jnp.mean(v, axis=-1, keepdims=True)
        vc = v - mu
        var = jnp.mean(vc * vc, axis=-1, keepdims=True)
        y = vc * lax.rsqrt(var + EPS) * lng_ref[...] + lnb_ref[...]
        cv_ref[t] = y * _sigmoid(y)
    nb_ref[...] = up_scr[t_len:t_len + nh]


def _conv_tm_call(proj3, buf_t, w_dw, b_dw, ln_g, ln_b, c, col0, bb):
    t_len, bsz, _ = proj3.shape
    nh = CONV_K - 1
    ja, jg = col0 // c, col0 // c + 1
    vec = lambda: pl.BlockSpec((1, c), lambda i: (0, 0))
    return pl.pallas_call(
        functools.partial(_conv_tm_kernel, col_blk=512),
        grid=(bsz // bb,),
        in_specs=[pl.BlockSpec((t_len, bb, c), lambda i: (0, i, ja)),
                  pl.BlockSpec((t_len, bb, c), lambda i: (0, i, jg)),
                  pl.BlockSpec((nh, bb, c), lambda i: (0, i, 0)),
                  pl.BlockSpec((CONV_K, c), lambda i: (0, 0)),
                  vec(), vec(), vec()],
        out_specs=[pl.BlockSpec((t_len, bb, c), lambda i: (0, i, 0)),
                   pl.BlockSpec((nh, bb, c), lambda i: (0, i, 0))],
        out_shape=[jax.ShapeDtypeStruct((t_len, bsz, c), F32),
                   jax.ShapeDtypeStruct((nh, bsz, c), F32)],
        scratch_shapes=[pltpu.VMEM((nh + t_len, bb, c), F32)],
        compiler_params=_cparams("arbitrary"),
        name="conv_tm",
    )(proj3, proj3, buf_t, w_dw, b_dw.reshape(1, c), ln_g.reshape(1, c), ln_b.reshape(1, c))


def _post_mix_kernel(x_ref, o_ref, g1_ref, sh2_ref, sc2_ref, npm_ref, npf_ref, x1_ref, h2_ref):
    def chunk(lead, sub, flat):
        x1 = x_ref[lead, sub, :] + _mod_rows(g1_ref, sub) * _rms(o_ref[flat, :], npm_ref[...])
        x1_ref[lead, sub, :] = x1
        h2 = _rms(x1, npf_ref[...]) * (1.0 + _mod_rows(sc2_ref, sub)) + _mod_rows(sh2_ref, sub)
        h2_ref[flat, :] = h2.astype(BF16)

    _for_row_chunks(x_ref.shape, chunk)


def _post_mix_call(x3, o2d, mod, npm, npf):
    n_rows, d = o2d.shape
    tm = NORM_ROW_TILE
    row = lambda: pl.BlockSpec((tm, d), lambda i: (i, 0))
    vec = lambda: pl.BlockSpec((1, d), lambda i: (0, 0))
    return pl.pallas_call(
        _post_mix_kernel,
        grid=(n_rows // tm,),
        in_specs=[mod.x_spec(tm), row(), mod.spec(2, tm), mod.spec(3, tm), mod.spec(4, tm), vec(), vec()],
        out_specs=[mod.x_spec(tm), row()],
        out_shape=[jax.ShapeDtypeStruct(x3.shape, F32), jax.ShapeDtypeStruct((n_rows, d), BF16)],
        compiler_params=_cparams("arbitrary"),
        name="post_mix",
    )(x3, o2d, mod.arr, mod.arr, mod.arr, npm.reshape(1, d), npf.reshape(1, d))


def _final_kernel(x1_ref, ff_ref, g2_ref, n_ref, o_ref):
    def chunk(lead, sub, flat):
        o_ref[lead, sub, :] = x1_ref[lead, sub, :] + _mod_rows(g2_ref, sub) * _rms(ff_ref[flat, :], n_ref[...])

    _for_row_chunks(x1_ref.shape, chunk)


def _final_call(x1, ff_all, row0, mod, npost):
    n_rows, d = x1.shape[0] * x1.shape[1], x1.shape[2]
    tm = NORM_ROW_TILE
    i0 = row0 // tm
    return pl.pallas_call(
        _final_kernel,
        grid=(n_rows // tm,),
        in_specs=[mod.x_spec(tm),
                  pl.BlockSpec((tm, d), lambda i: (i0 + i, 0)),
                  mod.spec(5, tm),
                  pl.BlockSpec((1, d), lambda i: (0, 0))],
        out_specs=mod.x_spec(tm),
        out_shape=jax.ShapeDtypeStruct(x1.shape, F32),
        compiler_params=_cparams("arbitrary"),
        name="final",
    )(x1, ff_all, mod.arr, npost.reshape(1, d))


def _extract_top(s, n_top):
    rows = s.shape[0]
    rid = lax.broadcasted_iota(jnp.int32, s.shape, 0).astype(F32)
    rank = jnp.full(s.shape, float(rows), F32)
    vals = []
    for it in range(n_top):
        m = jnp.max(s, axis=0, keepdims=True)
        first = jnp.min(jnp.where(s == m, rid, float(rows)), axis=0, keepdims=True)
        sel = rid == first
        rank = jnp.where(sel, float(it), rank)
        s = jnp.where(sel, -jnp.inf, s)
        vals.append(m)
    return vals, rank, jnp.zeros_like(vals[0])


def _extract_top_untied(s, n_top):
    rank = jnp.full(s.shape, float(s.shape[0]), F32)
    vals = []
    for it in range(n_top):
        m = jnp.max(s, axis=0, keepdims=True)
        sel = s == m
        rank = jnp.where(sel, float(it), rank)
        s = jnp.where(sel, -jnp.inf, s)
        vals.append(m)
    taken = jnp.sum(jnp.where(rank < float(n_top), 1.0, 0.0), axis=0, keepdims=True)
    return vals, rank, jnp.abs(taken - float(n_top))


def _route_tables(s0, s1, cand_scr, extract):
    nk = PK_NKEYS
    a_vals, rank0, tied0 = extract(s0, PK_TOPK)
    b_vals, rank1, tied1 = extract(s1, PK_TOPK)
    cand_scr[...] = jnp.full(cand_scr.shape, -jnp.inf, F32)
    for r, (k0, k1) in enumerate(PK_CANDS):
        cand_scr[r:r + 1, :] = a_vals[k0] + b_vals[k1]
    best, rank2, tied2 = extract(cand_scr[...], PK_TOPK)
    z = jnp.zeros_like(best[0])
    for v in best:
        z = z + jnp.exp(v - best[0])
    chosen = jnp.where(rank2 < float(PK_TOPK), 1.0, 0.0)
    cand_row = lax.broadcasted_iota(jnp.int32, (PK_CAND_ROWS, LANES), 0)
    cnt = jnp.zeros((nk, LANES), F32)
    r = 0
    for k0 in range(PK_TOPK):
        n_k0 = sum(1 for cand in PK_CANDS if cand[0] == k0)
        in_k0 = (cand_row >= r) & (cand_row < r + n_k0)
        cnt_k0 = jnp.sum(jnp.where(in_k0, chosen, 0.0), axis=0, keepdims=True)
        cnt = jnp.where(rank0 == float(k0), cnt_k0, cnt)
        r += n_k0
    return (rank1, jnp.exp(s1 - b_vals[0]), cnt, jnp.exp(s0 - a_vals[0]) / z), tied0 + tied1 + tied2


def _route_kernel(h2_ref, wq_ref, keys_ref, r1_ref, e1_ref, cnt_ref, e0_ref, cand_scr):
    nk = PK_NKEYS
    qt = lax.dot_general(wq_ref[...], h2_ref[...], (((1,), (1,)), ((), ())), preferred_element_type=F32)
    s0_all = jnp.dot(keys_ref[0, 0], qt[:nk], precision=HIGHEST, preferred_element_type=F32)
    s1_all = jnp.dot(keys_ref[0, 1], qt[nk:], precision=HIGHEST, preferred_element_type=F32)
    out_refs = (r1_ref, e1_ref, cnt_ref, e0_ref)
    n_chunks = h2_ref.shape[0] // LANES
    tied = []
    for c in range(n_chunks):
        sl = slice(c * LANES, (c + 1) * LANES)
        tables, tied_c = _route_tables(s0_all[:, sl], s1_all[:, sl], cand_scr.at[c], _extract_top_untied)
        tied.append(jnp.max(tied_c))
        for ref, tab in zip(out_refs, tables):
            ref[0, :, sl] = tab.astype(ref.dtype)
    for c in range(n_chunks):
        sl = slice(c * LANES, (c + 1) * LANES)

        @pl.when(tied[c] > 0.0)
        def _():
            exact, _ = _route_tables(s0_all[:, sl], s1_all[:, sl], cand_scr.at[c], _extract_top)
            for ref, tab in zip(out_refs, exact):
                ref[0, :, sl] = tab.astype(ref.dtype)


def _route_call(h2_all, wqt_b, keys):
    n_tok, d = h2_all.shape
    heads, _, nk, half = keys.shape
    tt = 512
    out = lambda: pl.BlockSpec((1, nk, tt), lambda i, h: (h, 0, i))
    tab = lambda dt: jax.ShapeDtypeStruct((heads, nk, n_tok), dt)
    return pl.pallas_call(
        _route_kernel,
        grid=(n_tok // tt, heads),
        in_specs=[pl.BlockSpec((tt, d), lambda i, h: (i, 0)),
                  pl.BlockSpec((2 * half, d), lambda i, h: (h, 0)),
                  pl.BlockSpec((1, 2, nk, half), lambda i, h: (h, 0, 0, 0))],
        out_specs=[out(), out(), out(), out()],
        out_shape=[tab(F32)] * 4,
        scratch_shapes=[pltpu.VMEM((tt // LANES, PK_CAND_ROWS, LANES), F32)],
        compiler_params=_cparams("arbitrary", "arbitrary"),
        name="peer_route",
    )(h2_all, wqt_b, keys)


def _experts_kernel(h2_ref, u_ref, v_ref, r1_ref, e1_ref, cnt_ref, e0_ref, o_ref, act_scr, w_scr):
    eb, tt = act_scr.shape
    nk = PK_NKEYS
    e = pl.program_id(1)

    @pl.when(e == 0)
    def _():
        o_ref[...] = jnp.zeros_like(o_ref)

    st = lax.dot_general(u_ref[...], h2_ref[...], (((1,), (1,)), ((), ())), preferred_element_type=F32)
    act_scr[...] = _gelu_tanh(st)
    for ii in range(eb // nk):
        row = pl.ds(e * (eb // nk) + ii, 1)
        cnt_rows = [cnt_ref[h, row, :] for h in range(PK_HEADS)]
        e0_rows = [e0_ref[h, row, :] for h in range(PK_HEADS)]
        for c in range(tt // LANES):
            sl = slice(c * LANES, (c + 1) * LANES)
            w = jnp.zeros((nk, LANES), F32)
            for h in range(PK_HEADS):
                w = w + jnp.where(r1_ref[h, :, sl] < cnt_rows[h][:, sl], e1_ref[h, :, sl] * e0_rows[h][:, sl], 0.0)
            w_scr[ii * nk:(ii + 1) * nk, sl] = (w * act_scr[ii * nk:(ii + 1) * nk, sl]).astype(BF16)
    o_ref[...] += lax.dot_general(w_scr[...], v_ref[...], (((0,), (0,)), ((), ())),
                                  preferred_element_type=F32)


def _experts_call(h2_all, u_b, v_b, r1, e1, cnt, e0):
    n_tok, d = h2_all.shape
    n_exp = u_b.shape[0]
    heads, nk, _ = r1.shape
    tt, eb = 512, 512
    once = pl.Buffered(1)
    aux = lambda: pl.BlockSpec((heads, nk, tt), lambda i, e: (0, 0, i), pipeline_mode=once)
    return pl.pallas_call(
        _experts_kernel,
        grid=(n_tok // tt, n_exp // eb),
        in_specs=[pl.BlockSpec((tt, d), lambda i, e: (i, 0), pipeline_mode=once),
                  pl.BlockSpec((eb, d), lambda i, e: (e, 0)),
                  pl.BlockSpec((eb, d), lambda i, e: (e, 0)),
                  aux(), aux(), aux(), aux()],
        out_specs=pl.BlockSpec((tt, d), lambda i, e: (i, 0)),
        out_shape=jax.ShapeDtypeStruct((n_tok, d), F32),
        scratch_shapes=[pltpu.VMEM((eb, tt), F32), pltpu.VMEM((eb, tt), BF16)],
        compiler_params=_cparams("arbitrary", "arbitrary"),
        name="peer_experts",
    )(h2_all, u_b, v_b, r1, e1, cnt, e0)


def _mixer(x3, mod, h0_re, h0_im, conv_buf, w):
    d = x3.shape[2]
    n_rows = x3.shape[0] * x3.shape[1]
    proj = _inproj_call(x3, mod, w["norm_pre_mix"], w["w_in_b"])
    proj3 = proj.reshape(x3.shape[0], x3.shape[1], -1)
    sw = w["ssm_b_glu"].shape[0]
    cw = w["conv_b_dw"].shape[0]

    t_len, bsz = (x3.shape[0], x3.shape[1]) if mod.time_major else (x3.shape[1], x3.shape[0])
    y_ssm, new_re, new_im = _ssm_branch(proj, bsz, t_len, mod.time_major, h0_re, h0_im, w["ssm"])
    z = _glu_call(y_ssm, w["w_glu_b"], w["ssm_b_glu"])
    y_a = _gated_proj_call(z, w["w_proj_b"], proj, sw + 2 * cw)

    conv_w = (w["conv_w_dw"], w["conv_b_dw"], w["conv_ln_g"], w["conv_ln_b"], cw, sw)
    if mod.time_major:
        cv, nb_t = _conv_tm_call(proj3, jnp.transpose(conv_buf, (1, 0, 2)), *conv_w, 16)
        new_buf = jnp.transpose(nb_t, (1, 0, 2))
    else:
        if conv_buf is None:
            conv_buf = jnp.zeros((x3.shape[0], CONV_K - 1, cw), F32)
        cv, new_buf = _conv_call(proj3, conv_buf, *conv_w, 1, 256)
    mix = _gated_proj_call(cv.reshape(n_rows, cw), w["w_pw_b"], proj, sw + 2 * cw + d, add=y_a, out_dtype=BF16)
    o = _matmul_call(mix, w["w_out_b"])
    x1, h2 = _post_mix_call(x3, o, mod, w["norm_post_mix"], w["norm_pre_ffn"])
    return x1, h2, new_re, new_im, new_buf


def _layer(xp, xs, c_prompt, c_sample, st_re, st_im, st_conv, w):
    bp, tp, d = xp.shape
    bs, ts, _ = xs.shape

    pad = (-(bp + bs)) % 8
    c_all = jnp.concatenate([c_prompt, c_sample, jnp.zeros((pad, d), F32)], axis=0)
    mod_all = _mod_call(c_all, w["w_mod"], w["b_mod"])
    mod_p = _Mod(mod_all[:bp].reshape(bp, 1, N_MOD * d), False, tp, d)
    mod_s = _Mod(mod_all[bp:bp + bs], True, bs, d)

    x1p, h2p, p_re, p_im, p_buf = _mixer(xp, mod_p, None, None, None, w)
    x1s, h2s, s_re, s_im, s_buf = _mixer(jnp.transpose(xs, (1, 0, 2)), mod_s, st_re, st_im, st_conv, w)

    h2_all = jnp.concatenate([h2p, h2s], axis=0)
    r1, e1, cnt, e0 = _route_call(h2_all, w["w_qt_b"], w["peer_sub_keys"])
    ff = _experts_call(h2_all, w["peer_u_b"], w["peer_v_b"], r1, e1, cnt, e0)

    yp = _final_call(x1p, ff, 0, mod_p, w["norm_post_ffn"])
    ys = jnp.transpose(_final_call(x1s, ff, bp * tp, mod_s, w["norm_post_ffn"]), (1, 0, 2))
    return yp, ys, p_re, p_im, p_buf, s_re, s_im, s_buf


def kernel(x_prompt, x_sample, c_prompt, c_sample, state_ssm_re, state_ssm_im, state_conv, w_mod, b_mod,
           norm_pre_mix, norm_post_mix, norm_pre_ffn, norm_post_ffn, w_in, ssm_lambda_re, ssm_lambda_im,
           ssm_log_dt, ssm_b_re, ssm_b_im, ssm_c_re, ssm_c_im, ssm_d, ssm_w_glu, ssm_b_glu, w_proj_ssm,
           conv_w_dw, conv_b_dw, conv_ln_g, conv_ln_b, conv_w_pw, w_out, peer_w_q, peer_sub_keys, peer_u,
           peer_v):
    depth = w_mod.shape[0]
    xp, xs = x_prompt, x_sample
    outs = [[] for _ in range(6)]
    for l in range(depth):
        w = {
            "w_mod": w_mod[l], "b_mod": b_mod[l],
            "norm_pre_mix": norm_pre_mix[l], "norm_post_mix": norm_post_mix[l],
            "norm_pre_ffn": norm_pre_ffn[l], "norm_post_ffn": norm_post_ffn[l],
            "w_in_b": w_in[l].astype(BF16),
            "ssm": (ssm_lambda_re[l], ssm_lambda_im[l], ssm_log_dt[l], ssm_b_re[l], ssm_b_im[l],
                    ssm_c_re[l], ssm_c_im[l], ssm_d[l]),
            "w_glu_b": ssm_w_glu[l].astype(BF16), "ssm_b_glu": ssm_b_glu[l],
            "w_proj_b": w_proj_ssm[l].astype(BF16),
            "conv_w_dw": conv_w_dw[l], "conv_b_dw": conv_b_dw[l],
            "conv_ln_g": conv_ln_g[l], "conv_ln_b": conv_ln_b[l],
            "w_pw_b": conv_w_pw[l].astype(BF16), "w_out_b": w_out[l].astype(BF16),
            "w_qt_b": jnp.transpose(peer_w_q[l]).astype(BF16),
            "peer_sub_keys": peer_sub_keys[l],
            "peer_u_b": peer_u[l].astype(BF16), "peer_v_b": peer_v[l].astype(BF16),
        }
        xp, xs, *states = _layer(xp, xs, c_prompt, c_sample, state_ssm_re[l], state_ssm_im[l], state_conv[l], w)
        for acc, s in zip(outs, states):
            acc.append(s)
    return (xp, xs) + tuple(jnp.stack(o) for o in outs)
```

```python
import functools
import math

import jax
import jax.numpy as jnp
from jax import lax
from jax.experimental import pallas as pl
from jax.experimental.pallas import tpu as pltpu

F32 = jnp.float32
BF16 = jnp.bfloat16
HIGHEST = lax.Precision.HIGHEST

EPS = 1e-6
N_MOD = 6
SSM_GROUP = 16
SSM_STATE = 64
CONV_K = 31
CONV_HIST = 32
PK_HEADS = 8
PK_NKEYS = 128
PK_TOPK = 16
SSM_CHUNK = 16

V7X_VMEM_BYTES = 64 * 1024 * 1024
VMEM_CAP_MB = 56
LANES = 128
SUBLANES = 8
ROW_TILE = 512
MID_ROW_TILE = 1024
NORM_ROW_TILE = 256
COL_TILE = 1024

PK_CANDS = tuple((k0, k1) for k0 in range(PK_TOPK) for k1 in range(PK_TOPK)
                 if (k0 + 1) * (k1 + 1) <= PK_TOPK)
PK_CAND_ROWS = 56


def _cparams(*sem):
    assert VMEM_CAP_MB * 1024 * 1024 <= V7X_VMEM_BYTES
    return pltpu.CompilerParams(dimension_semantics=sem, vmem_limit_bytes=VMEM_CAP_MB * 1024 * 1024)


def _sigmoid(x):
    return 1.0 / (1.0 + jnp.exp(-x))


def _gelu_tanh(x):
    k0 = 2.0 * math.sqrt(2.0 / math.pi)
    return x / (1.0 + jnp.exp(-(x * (k0 + (k0 * 0.044715) * (x * x)))))


def _rms(x, g):
    return x * lax.rsqrt(jnp.mean(x * x, axis=-1, keepdims=True) + EPS) * g


def _mod_kernel(c_ref, w_ref, b_ref, o_ref):
    c = c_ref[...]
    a = (c * _sigmoid(c)).astype(BF16)
    o_ref[...] = jnp.dot(a, w_ref[...].astype(BF16), preferred_element_type=F32) + b_ref[...]


def _mod_call(c_all, w_mod, b_mod):
    m, d = c_all.shape
    n = w_mod.shape[1]
    tn = 512
    return pl.pallas_call(
        _mod_kernel,
        grid=(n // tn,),
        in_specs=[pl.BlockSpec((m, d), lambda j: (0, 0)),
                  pl.BlockSpec((d, tn), lambda j: (0, j)),
                  pl.BlockSpec((1, tn), lambda j: (0, j))],
        out_specs=pl.BlockSpec((m, tn), lambda j: (0, j)),
        out_shape=jax.ShapeDtypeStruct((m, n), F32),
        compiler_params=_cparams("arbitrary"),
        name="mod",
    )(c_all, w_mod, b_mod.reshape(1, n))


class _Mod:
    def __init__(self, arr, time_major, seq, d):
        self.arr, self.time_major, self.seq, self.d = arr, time_major, seq, d

    def x_spec(self, rows):
        if self.time_major:
            assert rows % self.seq == 0
            return pl.BlockSpec((rows // self.seq, self.seq, self.d), lambda i, *_: (i, 0, 0))
        assert self.seq % rows == 0
        tpb = self.seq // rows
        return pl.BlockSpec((1, rows, self.d), lambda i, *_: (i // tpb, i % tpb, 0))

    def spec(self, k, rows):
        if self.time_major:
            return pl.BlockSpec((self.seq, self.d), lambda i, *_: (0, k))
        tpb = self.seq // rows
        return pl.BlockSpec((None, 1, self.d), lambda i, *_: (i // tpb, 0, k))


NORM_CHUNK_ROWS = 64


def _for_row_chunks(tile_shape, fn):
    n_lead, n_rows, _ = tile_shape
    cr = NORM_CHUNK_ROWS
    per_lead = n_rows // cr

    def body(c, carry):
        sub = pl.ds(pl.multiple_of((c % per_lead) * cr, cr), cr)
        fn(c // per_lead, sub, pl.ds(pl.multiple_of(c * cr, cr), cr))
        return carry

    lax.fori_loop(0, n_lead * per_lead, body, 0)


def _mod_rows(ref, sub):
    return ref[...] if ref.shape[0] == 1 else ref[sub, :]


def _inproj_kernel(x_ref, sh_ref, sc_ref, g_ref, w_ref, o_ref, h_scr):
    @pl.when(pl.program_id(1) == 0)
    def _():
        def chunk(lead, sub, flat):
            h = _rms(x_ref[lead, sub, :], g_ref[...]) * (1.0 + _mod_rows(sc_ref, sub)) + _mod_rows(sh_ref, sub)
            h_scr[flat, :] = h.astype(BF16)

        _for_row_chunks(x_ref.shape, chunk)

    o_ref[...] = jnp.dot(h_scr[...], w_ref[...], preferred_element_type=F32)


def _inproj_call(x3, mod, g, w_b):
    n_rows, d = x3.shape[0] * x3.shape[1], x3.shape[2]
    n = w_b.shape[1]
    tm, tn = ROW_TILE, COL_TILE
    return pl.pallas_call(
        _inproj_kernel,
        grid=(n_rows // tm, n // tn),
        in_specs=[mod.x_spec(tm), mod.spec(0, tm), mod.spec(1, tm),
                  pl.BlockSpec((1, d), lambda i, j: (0, 0)),
                  pl.BlockSpec((d, tn), lambda i, j: (0, j))],
        out_specs=pl.BlockSpec((tm, tn), lambda i, j: (i, j)),
        out_shape=jax.ShapeDtypeStruct((n_rows, n), F32),
        scratch_shapes=[pltpu.VMEM((tm, d), BF16)],
        compiler_params=_cparams("arbitrary", "arbitrary"),
        name="inproj",
    )(x3, mod.arr, mod.arr, g.reshape(1, d), w_b)


def _glu_kernel(y_ref, yt_ref, w_ref, b_ref, o_ref, a_scr):
    @pl.when(pl.program_id(1) == 0)
    def _():
        a_scr[...] = _gelu_tanh(y_ref[...]).astype(BF16)

    acc = jnp.dot(a_scr[...], w_ref[...], preferred_element_type=F32) + b_ref[...]
    o_ref[...] = (_gelu_tanh(yt_ref[...]) * _sigmoid(acc)).astype(BF16)


def _glu_call(y2d, w_b, b):
    n_rows, k = y2d.shape
    n = w_b.shape[1]
    tm, tn = min(MID_ROW_TILE, n_rows), COL_TILE
    return pl.pallas_call(
        _glu_kernel,
        grid=(n_rows // tm, n // tn),
        in_specs=[pl.BlockSpec((tm, k), lambda i, j: (i, 0)),
                  pl.BlockSpec((tm, tn), lambda i, j: (i, j)),
                  pl.BlockSpec((k, tn), lambda i, j: (0, j)),
                  pl.BlockSpec((1, tn), lambda i, j: (0, j))],
        out_specs=pl.BlockSpec((tm, tn), lambda i, j: (i, j)),
        out_shape=jax.ShapeDtypeStruct((n_rows, n), BF16),
        scratch_shapes=[pltpu.VMEM((tm, k), BF16)],
        compiler_params=_cparams("arbitrary", "arbitrary"),
        name="ssm_glu",
    )(y2d, y2d, w_b, b.reshape(1, n))


def _gated_proj_kernel(z_ref, w_ref, gate_ref, o_ref):
    acc = jnp.dot(z_ref[...].astype(BF16), w_ref[...], preferred_element_type=F32)
    o_ref[...] = (_sigmoid(gate_ref[...]) * acc).astype(o_ref.dtype)


def _gated_add_proj_kernel(z_ref, w_ref, gate_ref, a_ref, o_ref):
    acc = jnp.dot(z_ref[...].astype(BF16), w_ref[...], preferred_element_type=F32)
    o_ref[...] = (a_ref[...] + _sigmoid(gate_ref[...]) * acc).astype(o_ref.dtype)


def _gated_proj_call(z2d, w_b, proj, gate_col0, add=None, out_dtype=F32):
    n_rows, k = z2d.shape
    n = w_b.shape[1]
    tm, tn = min(MID_ROW_TILE, n_rows), COL_TILE
    gj = gate_col0 // tn
    in_specs = [pl.BlockSpec((tm, k), lambda i, j: (i, 0)),
                pl.BlockSpec((k, tn), lambda i, j: (0, j)),
                pl.BlockSpec((tm, tn), lambda i, j: (i, gj + j))]
    args = [z2d, w_b, proj]
    if add is not None:
        in_specs.append(pl.BlockSpec((tm, tn), lambda i, j: (i, j)))
        args.append(add)
    return pl.pallas_call(
        _gated_proj_kernel if add is None else _gated_add_proj_kernel,
        grid=(n_rows // tm, n // tn),
        in_specs=in_specs,
        out_specs=pl.BlockSpec((tm, tn), lambda i, j: (i, j)),
        out_shape=jax.ShapeDtypeStruct((n_rows, n), out_dtype),
        compiler_params=_cparams("arbitrary", "arbitrary"),
        name="gated_proj" if add is None else "gated_add_proj",
    )(*args)


def _matmul_kernel(a_ref, w_ref, o_ref):
    o_ref[...] = jnp.dot(a_ref[...], w_ref[...], preferred_element_type=F32)


def _matmul_call(a2d, w_b):
    n_rows, k = a2d.shape
    n = w_b.shape[1]
    tm, tn = min(MID_ROW_TILE, n_rows), COL_TILE
    return pl.pallas_call(
        _matmul_kernel,
        grid=(n_rows // tm, n // tn),
        in_specs=[pl.BlockSpec((tm, k), lambda i, j: (i, 0)),
                  pl.BlockSpec((k, tn), lambda i, j: (0, j))],
        out_specs=pl.BlockSpec((tm, tn), lambda i, j: (i, j)),
        out_shape=jax.ShapeDtypeStruct((n_rows, n), F32),
        compiler_params=_cparams("arbitrary", "arbitrary"),
        name="out_proj",
    )(a2d, w_b)


def _ssm_param_kernel(lrc_ref, lic_ref, lrr_ref, lir_ref, ldt_ref, ctre_ref, ctim_ref, btre_ref, btim_ref,
                      dpad_ref, t_ref, sre_ref, sim_ref, rre_ref, rim_ref, lre_ref, lim_ref, *, chunk):
    p, n = SSM_GROUP, SSM_STATE
    lp = chunk * p
    sre_ref[...] = jnp.zeros_like(sre_ref)
    sim_ref[...] = jnp.zeros_like(sim_ref)
    rre_ref[...] = jnp.zeros_like(rre_ref)
    rim_ref[...] = jnp.zeros_like(rim_ref)
    lane = lax.broadcasted_iota(jnp.int32, (p, lp), 1)
    diag = lax.broadcasted_iota(jnp.int32, (p, lp), 0) == lane
    k_lane = (lax.broadcasted_iota(jnp.int32, (n, lp), 1) // p).astype(F32)
    e_row = (chunk - 1 - lax.broadcasted_iota(jnp.int32, (chunk, n), 0)).astype(F32)
    for a in range(2):
        dt = jnp.exp(ldt_ref[a])
        lrc, lic = lrc_ref[a] * dt, lic_ref[a] * dt
        lrr, lir = lrr_ref[a], lir_ref[a]
        m0 = jnp.exp(lrc * k_lane)
        p0re, p0im = m0 * jnp.cos(lic * k_lane), m0 * jnp.sin(lic * k_lane)
        lbre, lbim = jnp.exp(lrc) * jnp.cos(lic), jnp.exp(lrc) * jnp.sin(lic)
        p1re, p1im = p0re * lbre - p0im * lbim, p0re * lbim + p0im * lbre
        ctre, ctim = ctre_ref[a], ctim_ref[a]
        dec = jnp.exp(lrr * dt)
        abre, abim = dec * jnp.cos(lir * dt), dec * jnp.sin(lir * dt)
        den = lrr * lrr + lir * lir
        fre = ((abre - 1.0) * lrr + abim * lir) / den
        fim = (abim * lrr - (abre - 1.0) * lir) / den
        bre = fre * btre_ref[a] - fim * btim_ref[a]
        bim = fre * btim_ref[a] + fim * btre_ref[a]
        clre = ctre * p0re - ctim * p0im
        clim = ctre * p0im + ctim * p0re
        kt = (jnp.dot(bre, clre, precision=HIGHEST, preferred_element_type=F32)
              - jnp.dot(bim, clim, precision=HIGHEST, preferred_element_type=F32))
        kt = kt + jnp.where(diag, dpad_ref[a], 0.0)
        for s in range(chunk):
            blk = kt if s == 0 else jnp.where(lane >= s * p, pltpu.roll(kt, s * p, 1), 0.0)
            t_ref[a, s * p:(s + 1) * p, :] = blk.astype(BF16)
        mr = jnp.exp(lrr * dt * e_row)
        prre, prim = mr * jnp.cos(lir * dt * e_row), mr * jnp.sin(lir * dt * e_row)
        for s in range(chunk):
            pr, pi = prre[s:s + 1, :], prim[s:s + 1, :]
            r0 = a * lp + s * p
            sre_ref[0, r0:r0 + p, a * n:(a + 1) * n] = bre * pr - bim * pi
            sim_ref[0, r0:r0 + p, a * n:(a + 1) * n] = bre * pi + bim * pr
        rre_ref[0, a * n:(a + 1) * n, a * lp:(a + 1) * lp] = (ctre * p1re - ctim * p1im).astype(BF16)
        rim_ref[0, a * n:(a + 1) * n, a * lp:(a + 1) * lp] = (-(ctre * p1im + ctim * p1re)).astype(BF16)
        ml = jnp.exp(lrr * dt * float(chunk))
        lre_ref[0, :, a * n:(a + 1) * n] = ml * jnp.cos(lir * dt * float(chunk))
        lim_ref[0, :, a * n:(a + 1) * n] = ml * jnp.sin(lir * dt * float(chunk))


def _ssm_param_call(lam_re, lam_im, log_dt, b_re, b_im, c_re, c_im, d_skip, chunk):
    g, n = lam_re.shape
    p = SSM_GROUP
    lp = chunk * p
    ct_re = jnp.tile(jnp.transpose(c_re, (0, 2, 1)), (1, 1, chunk))
    ct_im = jnp.tile(jnp.transpose(c_im, (0, 2, 1)), (1, 1, chunk))
    bt_re = jnp.transpose(b_re, (0, 2, 1))
    bt_im = jnp.transpose(b_im, (0, 2, 1))
    dpad = jnp.pad(d_skip.reshape(g, 1, p), ((0, 0), (0, 0), (0, lp - p)))
    spec3 = lambda s: pl.BlockSpec((2,) + s, lambda i: (i, 0, 0))
    pair = lambda s: pl.BlockSpec((1,) + s, lambda i: (i, 0, 0))
    return pl.pallas_call(
        functools.partial(_ssm_param_kernel, chunk=chunk),
        grid=(g // 2,),
        in_specs=[spec3((n, 1)), spec3((n, 1)), spec3((1, n)), spec3((1, n)), spec3((1, 1)),
                  spec3((n, lp)), spec3((n, lp)), spec3((p, n)), spec3((p, n)), spec3((1, lp))],
        out_specs=[spec3((lp, lp)), pair((2 * lp, 2 * n)), pair((2 * lp, 2 * n)),
                   pair((2 * n, 2 * lp)), pair((2 * n, 2 * lp)), pair((1, 2 * n)), pair((1, 2 * n))],
        out_shape=[jax.ShapeDtypeStruct((g, lp, lp), BF16),
                   jax.ShapeDtypeStruct((g // 2, 2 * lp, 2 * n), F32),
                   jax.ShapeDtypeStruct((g // 2, 2 * lp, 2 * n), F32),
                   jax.ShapeDtypeStruct((g // 2, 2 * n, 2 * lp), BF16),
                   jax.ShapeDtypeStruct((g // 2, 2 * n, 2 * lp), BF16),
                   jax.ShapeDtypeStruct((g // 2, 1, 2 * n), F32),
                   jax.ShapeDtypeStruct((g // 2, 1, 2 * n), F32)],
        compiler_params=_cparams("arbitrary"),
        name="ssm_params",
    )(lam_re.reshape(g, n, 1), lam_im.reshape(g, n, 1), lam_re.reshape(g, 1, n), lam_im.reshape(g, 1, n),
      log_dt.reshape(g, 1, 1), ct_re, ct_im, bt_re, bt_im, dpad)


SSM_LANE_GROUPS = LANES // SSM_GROUP
SSM_ROW_BLK = 16


def _transpose_lane_blocks(load_src, store_dst):
    n = SSM_LANE_GROUPS
    tiles = [load_src(i) for i in range(n)]
    lane_blk = lax.broadcasted_iota(jnp.int32, tiles[0].shape, 1) // SSM_GROUP
    d = 1
    while d < n:
        upper = (lane_blk & d) != 0
        nxt = list(tiles)
        for a in range(n):
            if a & d == 0:
                ta, tb = tiles[a], tiles[a + d]
                nxt[a] = jnp.where(upper, pltpu.roll(tb, d * SSM_GROUP, 1), ta)
                nxt[a + d] = jnp.where(upper, tb, pltpu.roll(ta, LANES - d * SSM_GROUP, 1))
        tiles = nxt
        d *= 2
    for dst in range(n):
        store_dst(dst, tiles[dst])


def _ssm_main_kernel(u_ref, t_ref, sre_ref, sim_ref, rre_ref, rim_ref, lre_ref, lim_ref, h0re_ref, h0im_ref,
                     y_ref, hnre_ref, hnim_ref, x_scr, yg_scr, vre_scr, vim_scr, hre_scr, him_scr,
                     *, batch, chunks, chunk, time_major):
    rows = batch * chunks
    lp = chunk * SSM_GROUP
    rb = SSM_ROW_BLK

    def token_rows(r0, step):
        if time_major:
            return pl.ds(step * batch + r0, rb)
        return pl.ds(r0 * chunk + step, rb, stride=chunk)

    def regroup(i, carry):
        r0 = pl.multiple_of(i * rb, rb)
        for cb in range(lp // LANES):
            lanes = slice(cb * LANES, (cb + 1) * LANES)
            steps = cb * SSM_LANE_GROUPS

            def store_x(g, val):
                x_scr[g, pl.ds(r0, rb), lanes] = val

            _transpose_lane_blocks(lambda s8: u_ref[token_rows(r0, steps + s8), :], store_x)
        return carry

    lax.fori_loop(0, rows // rb, regroup, 0, unroll=2)

    def pair(pr, carry):
        u0, u1 = x_scr[2 * pr], x_scr[2 * pr + 1]
        ucat = jnp.concatenate([u0, u1], axis=1)
        vre_scr[...] = jnp.dot(ucat, sre_ref[pr], precision=HIGHEST, preferred_element_type=F32)
        vim_scr[...] = jnp.dot(ucat, sim_ref[pr], precision=HIGHEST, preferred_element_type=F32)
        are, aim = lre_ref[pr], lim_ref[pr]
        if chunks == 1:
            hre, him = h0re_ref[pr], h0im_ref[pr]
            hre_scr[...] = hre
            him_scr[...] = him
            hnre_ref[pr] = are * hre - aim * him + vre_scr[...]
            hnim_ref[pr] = are * him + aim * hre + vim_scr[...]
        else:
            init = (tuple(h0re_ref[pr, b:b + 1, :] for b in range(batch))
                    + tuple(h0im_ref[pr, b:b + 1, :] for b in range(batch)))

            def body(m, hs):
                nre, nim = [], []
                for b in range(batch):
                    hr, hi = hs[b], hs[batch + b]
                    row = b * chunks + m
                    hre_scr[pl.ds(row, 1), :] = hr
                    him_scr[pl.ds(row, 1), :] = hi
                    nre.append(are * hr - aim * hi + vre_scr[pl.ds(row, 1), :])
                    nim.append(are * hi + aim * hr + vim_scr[pl.ds(row, 1), :])
                return tuple(nre) + tuple(nim)

            fin = lax.fori_loop(0, chunks, body, init)
            for b in range(batch):
                hnre_ref[pr, b:b + 1, :] = fin[b]
                hnim_ref[pr, b:b + 1, :] = fin[batch + b]
        yh = (jnp.dot(hre_scr[...].astype(BF16), rre_ref[pr], preferred_element_type=F32)
              + jnp.dot(him_scr[...].astype(BF16), rim_ref[pr], preferred_element_type=F32))
        yg_scr[2 * pr] = jnp.dot(u0.astype(BF16), t_ref[2 * pr], preferred_element_type=F32) + yh[:, :lp]
        yg_scr[2 * pr + 1] = jnp.dot(u1.astype(BF16), t_ref[2 * pr + 1], preferred_element_type=F32) + yh[:, lp:]
        return carry

    lax.fori_loop(0, SSM_LANE_GROUPS // 2, pair, 0)

    def ungroup(i, carry):
        r0 = pl.multiple_of(i * rb, rb)
        for cb in range(lp // LANES):
            lanes = slice(cb * LANES, (cb + 1) * LANES)
            steps = cb * SSM_LANE_GROUPS

            def store_y(t8, val):
                y_ref[token_rows(r0, steps + t8), :] = val

            _transpose_lane_blocks(lambda g: yg_scr[g, pl.ds(r0, rb), lanes], store_y)
        return carry

    lax.fori_loop(0, rows // rb, ungroup, 0, unroll=2)


def _ssm_main_call(proj, ops, h0re, h0im, batch, chunks, chunk, time_major):
    t_op, s_re, s_im, r_re, r_im, l_re, l_im = ops
    n_tok = proj.shape[0]
    g, lp, _ = t_op.shape
    rows = batch * chunks
    assert rows % SSM_ROW_BLK == 0
    n2 = 2 * SSM_STATE
    ng = SSM_LANE_GROUPS
    grp = lambda s: pl.BlockSpec((ng,) + s, lambda j: (j, 0, 0))
    pair = lambda s: pl.BlockSpec((ng // 2,) + s, lambda j: (j, 0, 0))
    tok = lambda: pl.BlockSpec((n_tok, LANES), lambda j: (0, j))
    return pl.pallas_call(
        functools.partial(_ssm_main_kernel, batch=batch, chunks=chunks, chunk=chunk, time_major=time_major),
        grid=(g // ng,),
        in_specs=[tok(), grp((lp, lp)), pair((2 * lp, n2)), pair((2 * lp, n2)),
                  pair((n2, 2 * lp)), pair((n2, 2 * lp)), pair((1, n2)), pair((1, n2)),
                  pair((batch, n2)), pair((batch, n2))],
        out_specs=[tok(), pair((batch, n2)), pair((batch, n2))],
        out_shape=[jax.ShapeDtypeStruct((n_tok, g * SSM_GROUP), F32),
                   jax.ShapeDtypeStruct((g // 2, batch, n2), F32),
                   jax.ShapeDtypeStruct((g // 2, batch, n2), F32)],
        scratch_shapes=[pltpu.VMEM((ng, rows, lp), F32)] * 2 + [pltpu.VMEM((rows, n2), F32)] * 4,
        compiler_params=_cparams("arbitrary"),
        name="ssm_main",
    )(proj, t_op, s_re, s_im, r_re, r_im, l_re, l_im, h0re, h0im)


def _ssm_branch(proj, bsz, t_len, time_major, h0_re, h0_im, ssm_w):
    lam_re, lam_im, log_dt, b_re, b_im, c_re, c_im, d_skip = ssm_w
    g, n = lam_re.shape
    chunk = min(SSM_CHUNK, t_len)
    assert t_len % chunk == 0 and (chunk * SSM_GROUP) % LANES == 0
    chunks = t_len // chunk
    assert not time_major or chunks == 1
    ops = _ssm_param_call(lam_re, lam_im, log_dt, b_re, b_im, c_re, c_im, d_skip, chunk)
    if h0_re is None:
        h0re = jnp.zeros((g // 2, bsz, 2 * n), F32)
        h0im = h0re
    else:
        h0re = jnp.transpose(h0_re.reshape(bsz, g // 2, 2 * n), (1, 0, 2))
        h0im = jnp.transpose(h0_im.reshape(bsz, g // 2, 2 * n), (1, 0, 2))
    y, hn_re, hn_im = _ssm_main_call(proj, ops, h0re, h0im, bsz, chunks, chunk, time_major)
    new_re = jnp.transpose(hn_re, (1, 0, 2)).reshape(bsz, g, n)
    new_im = jnp.transpose(hn_im, (1, 0, 2)).reshape(bsz, g, n)
    return y, new_re, new_im


def _conv_kernel(a_ref, g_ref, buf_ref, w_ref, bdw_ref, lng_ref, lnb_ref, cv_ref, nb_ref, up_scr, sh_scr, acc_scr,
                 *, row_blk, col_blk):
    bb, tt, c = a_ref.shape
    hist = CONV_HIST
    off = hist - (CONV_K - 1)
    j = pl.program_id(1)

    @pl.when(j == 0)
    def _():
        up_scr[:, off:hist, :] = buf_ref[...]

    up_scr[:, hist:hist + tt, :] = a_ref[...] * _sigmoid(g_ref[...])
    n_sh = sh_scr.shape[2]
    for r in range(1, SUBLANES):
        sh_scr[r - 1] = up_scr[:, r:r + n_sh, :]
    for r0 in range(0, tt, row_blk):
        for c0 in range(0, c, col_blk):
            acc = jnp.zeros((bb, row_blk, col_blk), F32) + bdw_ref[:, c0:c0 + col_blk][None]
            for k in range(CONV_K):
                r = (k + off) % SUBLANES
                base = r0 + k + off - r
                if r == 0:
                    tap = up_scr[:, base:base + row_blk, c0:c0 + col_blk]
                else:
                    tap = sh_scr[r - 1, :, base:base + row_blk, c0:c0 + col_blk]
                acc = acc + tap * w_ref[k:k + 1, c0:c0 + col_blk][None]
            acc_scr[:, r0:r0 + row_blk, c0:c0 + col_blk] = acc
    v = acc_scr[...]
    mu = jnp.mean(v, axis=-1, keepdims=True)
    vc = v - mu
    var = jnp.mean(vc * vc, axis=-1, keepdims=True)
    y = vc * lax.rsqrt(var + EPS) * lng_ref[...][None] + lnb_ref[...][None]
    cv_ref[...] = (y * _sigmoid(y)).astype(cv_ref.dtype)
    tail = up_scr[:, tt + off:tt + hist, :]

    @pl.when(j == pl.num_programs(1) - 1)
    def _():
        nb_ref[...] = tail

    up_scr[:, off:hist, :] = tail


def _conv_call(proj3, buf, w_dw, b_dw, ln_g, ln_b, c, col0, bb, tt):
    bsz, t_len, _ = proj3.shape
    ja, jg = col0 // c, col0 // c + 1
    row_blk = min(tt, 64)
    vec = lambda: pl.BlockSpec((1, c), lambda i, j: (0, 0))
    return pl.pallas_call(
        functools.partial(_conv_kernel, row_blk=row_blk, col_blk=256),
        grid=(bsz // bb, t_len // tt),
        in_specs=[pl.BlockSpec((bb, tt, c), lambda i, j: (i, j, ja)),
                  pl.BlockSpec((bb, tt, c), lambda i, j: (i, j, jg)),
                  pl.BlockSpec((bb, CONV_K - 1, c), lambda i, j: (i, 0, 0)),
                  pl.BlockSpec((CONV_K, c), lambda i, j: (0, 0)),
                  vec(), vec(), vec()],
        out_specs=[pl.BlockSpec((bb, tt, c), lambda i, j: (i, j, 0)),
                   pl.BlockSpec((bb, CONV_K - 1, c), lambda i, j: (i, 0, 0))],
        out_shape=[jax.ShapeDtypeStruct((bsz, t_len, c), BF16),
                   jax.ShapeDtypeStruct((bsz, CONV_K - 1, c), F32)],
        scratch_shapes=[pltpu.VMEM((bb, CONV_HIST + tt, c), F32),
                        pltpu.VMEM((SUBLANES - 1, bb, CONV_HIST + tt - SUBLANES, c), F32),
                        pltpu.VMEM((bb, tt, c), F32)],
        compiler_params=_cparams("arbitrary", "arbitrary"),
        name="conv",
    )(proj3, proj3, buf, w_dw, b_dw.reshape(1, c), ln_g.reshape(1, c), ln_b.reshape(1, c))


def _conv_tm_kernel(a_ref, g_ref, buf_ref, w_ref, bdw_ref, lng_ref, lnb_ref, cv_ref, nb_ref, up_scr, acc_scr,
                    *, col_blk):
    t_len, bb, c = a_ref.shape
    nh = CONV_K - 1
    up_scr[0:nh] = buf_ref[...]
    up_scr[nh:nh + t_len] = a_ref[...] * _sigmoid(g_ref[...])
    for t in range(t_len):
        for c0 in range(0, c, col_blk):
            acc = jnp.zeros((bb, col_blk), F32) + bdw_ref[:, c0:c0 + col_blk]
            for k in range(CONV_K):
                acc = acc + up_scr[t + k, :, c0:c0 + col_blk] * w_ref[k:k + 1, c0:c0 + col_blk]
            acc_scr[t, :, c0:c0 + col_blk] = acc
    for t in range(t_len):
        v = acc_scr[t]
        mu = jnp.mean(v, axis=-1, keepdims=True)
        vc = v - mu
        var = jnp.mean(vc * vc, axis=-1, keepdims=True)
        y = vc * lax.rsqrt(var + EPS) * lng_ref[...] + lnb_ref[...]
        cv_ref[t] = (y * _sigmoid(y)).astype(cv_ref.dtype)
    nb_ref[...] = up_scr[t_len:t_len + nh]


def _conv_tm_call(proj3, buf_t, w_dw, b_dw, ln_g, ln_b, c, col0, bb):
    t_len, bsz, _ = proj3.shape
    nh = CONV_K - 1
    ja, jg = col0 // c, col0 // c + 1
    vec = lambda: pl.BlockSpec((1, c), lambda i: (0, 0))
    return pl.pallas_call(
        functools.partial(_conv_tm_kernel, col_blk=512),
        grid=(bsz // bb,),
        in_specs=[pl.BlockSpec((t_len, bb, c), lambda i: (0, i, ja)),
                  pl.BlockSpec((t_len, bb, c), lambda i: (0, i, jg)),
                  pl.BlockSpec((nh, bb, c), lambda i: (0, i, 0)),
                  pl.BlockSpec((CONV_K, c), lambda i: (0, 0)),
                  vec(), vec(), vec()],
        out_specs=[pl.BlockSpec((t_len, bb, c), lambda i: (0, i, 0)),
                   pl.BlockSpec((nh, bb, c), lambda i: (0, i, 0))],
        out_shape=[jax.ShapeDtypeStruct((t_len, bsz, c), BF16),
                   jax.ShapeDtypeStruct((nh, bsz, c), F32)],
        scratch_shapes=[pltpu.VMEM((nh + t_len, bb, c), F32), pltpu.VMEM((t_len, bb, c), F32)],
        compiler_params=_cparams("arbitrary"),
        name="conv_tm",
    )(proj3, proj3, buf_t, w_dw, b_dw.reshape(1, c), ln_g.reshape(1, c), ln_b.reshape(1, c))


def _post_mix_kernel(x_ref, o_ref, g1_ref, sh2_ref, sc2_ref, npm_ref, npf_ref, x1_ref, h2_ref):
    def chunk(lead, sub, flat):
        x1 = x_ref[lead, sub, :] + _mod_rows(g1_ref, sub) * _rms(o_ref[flat, :], npm_ref[...])
        x1_ref[lead, sub, :] = x1
        h2 = _rms(x1, npf_ref[...]) * (1.0 + _mod_rows(sc2_ref, sub)) + _mod_rows(sh2_ref, sub)
        h2_ref[flat, :] = h2.astype(BF16)

    _for_row_chunks(x_ref.shape, chunk)


def _post_mix_call(x3, o2d, mod, npm, npf):
    n_rows, d = o2d.shape
    tm = NORM_ROW_TILE
    row = lambda: pl.BlockSpec((tm, d), lambda i: (i, 0))
    vec = lambda: pl.BlockSpec((1, d), lambda i: (0, 0))
    return pl.pallas_call(
        _post_mix_kernel,
        grid=(n_rows // tm,),
        in_specs=[mod.x_spec(tm), row(), mod.spec(2, tm), mod.spec(3, tm), mod.spec(4, tm), vec(), vec()],
        out_specs=[mod.x_spec(tm), row()],
        out_shape=[jax.ShapeDtypeStruct(x3.shape, F32), jax.ShapeDtypeStruct((n_rows, d), BF16)],
        compiler_params=_cparams("arbitrary"),
        name="post_mix",
    )(x3, o2d, mod.arr, mod.arr, mod.arr, npm.reshape(1, d), npf.reshape(1, d))


def _final_kernel(x1_ref, ff_ref, g2_ref, n_ref, o_ref):
    def chunk(lead, sub, flat):
        o_ref[lead, sub, :] = x1_ref[lead, sub, :] + _mod_rows(g2_ref, sub) * _rms(ff_ref[flat, :], n_ref[...])

    _for_row_chunks(x1_ref.shape, chunk)


def _final_call(x1, ff_all, row0, mod, npost):
    n_rows, d = x1.shape[0] * x1.shape[1], x1.shape[2]
    tm = NORM_ROW_TILE
    i0 = row0 // tm
    return pl.pallas_call(
        _final_kernel,
        grid=(n_rows // tm,),
        in_specs=[mod.x_spec(tm),
                  pl.BlockSpec((tm, d), lambda i: (i0 + i, 0)),
                  mod.spec(5, tm),
                  pl.BlockSpec((1, d), lambda i: (0, 0))],
        out_specs=mod.x_spec(tm),
        out_shape=jax.ShapeDtypeStruct(x1.shape, F32),
        compiler_params=_cparams("arbitrary"),
        name="final",
    )(x1, ff_all, mod.arr, npost.reshape(1, d))


def _extract_top(s, n_top):
    rows = s.shape[0]
    rid = lax.broadcasted_iota(jnp.int32, s.shape, 0).astype(F32)
    rank = jnp.full(s.shape, float(rows), F32)
    vals = []
    for it in range(n_top):
        m = jnp.max(s, axis=0, keepdims=True)
        first = jnp.min(jnp.where(s == m, rid, float(rows)), axis=0, keepdims=True)
        sel = rid == first
        rank = jnp.where(sel, float(it), rank)
        s = jnp.where(sel, -jnp.inf, s)
        vals.append(m)
    return vals, rank, jnp.zeros_like(vals[0])


def _extract_top_untied(s, n_top):
    rank = jnp.full(s.shape, float(s.shape[0]), F32)
    vals = []
    for it in range(n_top):
        m = jnp.max(s, axis=0, keepdims=True)
        sel = s == m
        rank = jnp.where(sel, float(it), rank)
        s = jnp.where(sel, -jnp.inf, s)
        vals.append(m)
    taken = jnp.sum(jnp.where(rank < float(n_top), 1.0, 0.0), axis=0, keepdims=True)
    return vals, rank, jnp.abs(taken - float(n_top))


def _route_tables(s0, s1, cand_scr, extract):
    nk = PK_NKEYS
    a_vals, rank0, tied0 = extract(s0, PK_TOPK)
    b_vals, rank1, tied1 = extract(s1, PK_TOPK)
    cand_scr[...] = jnp.full(cand_scr.shape, -jnp.inf, F32)
    for r, (k0, k1) in enumerate(PK_CANDS):
        cand_scr[r:r + 1, :] = a_vals[k0] + b_vals[k1]
    best, rank2, tied2 = extract(cand_scr[...], PK_TOPK)
    z = jnp.zeros_like(best[0])
    for v in best:
        z = z + jnp.exp(v - best[0])
    chosen = jnp.where(rank2 < float(PK_TOPK), 1.0, 0.0)
    cand_row = lax.broadcasted_iota(jnp.int32, (PK_CAND_ROWS, LANES), 0)
    cnt = jnp.zeros((nk, LANES), F32)
    r = 0
    for k0 in range(PK_TOPK):
        n_k0 = sum(1 for cand in PK_CANDS if cand[0] == k0)
        in_k0 = (cand_row >= r) & (cand_row < r + n_k0)
        cnt_k0 = jnp.sum(jnp.where(in_k0, chosen, 0.0), axis=0, keepdims=True)
        cnt = jnp.where(rank0 == float(k0), cnt_k0, cnt)
        r += n_k0
    return (rank1, jnp.exp(s1 - b_vals[0]), cnt, jnp.exp(s0 - a_vals[0]) / z), tied0 + tied1 + tied2


def _route_kernel(h2_ref, wq_ref, keys_ref, r1_ref, e1_ref, cnt_ref, e0_ref, cand_scr):
    nk = PK_NKEYS
    qt = lax.dot_general(wq_ref[...], h2_ref[...], (((1,), (1,)), ((), ())), preferred_element_type=F32)
    s0_all = jnp.dot(keys_ref[0, 0], qt[:nk], precision=HIGHEST, preferred_element_type=F32)
    s1_all = jnp.dot(keys_ref[0, 1], qt[nk:], precision=HIGHEST, preferred_element_type=F32)
    out_refs = (r1_ref, e1_ref, cnt_ref, e0_ref)
    n_chunks = h2_ref.shape[0] // LANES
    tied = []
    for c in range(n_chunks):
        sl = slice(c * LANES, (c + 1) * LANES)
        tables, tied_c = _route_tables(s0_all[:, sl], s1_all[:, sl], cand_scr.at[c], _extract_top_untied)
        tied.append(jnp.max(tied_c))
        for ref, tab in zip(out_refs, tables):
            ref[0, :, sl] = tab.astype(ref.dtype)
    for c in range(n_chunks):
        sl = slice(c * LANES, (c + 1) * LANES)

        @pl.when(tied[c] > 0.0)
        def _():
            exact, _ = _route_tables(s0_all[:, sl], s1_all[:, sl], cand_scr.at[c], _extract_top)
            for ref, tab in zip(out_refs, exact):
                ref[0, :, sl] = tab.astype(ref.dtype)


def _route_call(h2_all, wqt_b, keys):
    n_tok, d = h2_all.shape
    heads, _, nk, half = keys.shape
    tt = 512
    out = lambda: pl.BlockSpec((1, nk, tt), lambda i, h: (h, 0, i))
    tab = lambda dt: jax.ShapeDtypeStruct((heads, nk, n_tok), dt)
    return pl.pallas_call(
        _route_kernel,
        grid=(n_tok // tt, heads),
        in_specs=[pl.BlockSpec((tt, d), lambda i, h: (i, 0)),
                  pl.BlockSpec((2 * half, d), lambda i, h: (h, 0)),
                  pl.BlockSpec((1, 2, nk, half), lambda i, h: (h, 0, 0, 0))],
        out_specs=[out(), out(), out(), out()],
        out_shape=[tab(F32)] * 4,
        scratch_shapes=[pltpu.VMEM((tt // LANES, PK_CAND_ROWS, LANES), F32)],
        compiler_params=_cparams("arbitrary", "arbitrary"),
        name="peer_route",
    )(h2_all, wqt_b, keys)


def _experts_kernel(h2_ref, u_ref, v_ref, r1_ref, e1_ref, cnt_ref, e0_ref, o_ref, act_scr, w_scr):
    eb, tt = act_scr.shape
    nk = PK_NKEYS
    e = pl.program_id(1)

    @pl.when(e == 0)
    def _():
        o_ref[...] = jnp.zeros_like(o_ref)

    st = lax.dot_general(u_ref[...], h2_ref[...], (((1,), (1,)), ((), ())), preferred_element_type=F32)
    act_scr[...] = _gelu_tanh(st)
    for ii in range(eb // nk):
        row = pl.ds(e * (eb // nk) + ii, 1)
        cnt_rows = [cnt_ref[h, row, :] for h in range(PK_HEADS)]
        e0_rows = [e0_ref[h, row, :] for h in range(PK_HEADS)]
        for c in range(tt // LANES):
            sl = slice(c * LANES, (c + 1) * LANES)
            w = jnp.zeros((nk, LANES), F32)
            for h in range(PK_HEADS):
                w = w + jnp.where(r1_ref[h, :, sl] < cnt_rows[h][:, sl], e1_ref[h, :, sl] * e0_rows[h][:, sl], 0.0)
            w_scr[ii * nk:(ii + 1) * nk, sl] = (w * act_scr[ii * nk:(ii + 1) * nk, sl]).astype(BF16)
    o_ref[...] += lax.dot_general(w_scr[...], v_ref[...], (((0,), (0,)), ((), ())),
                                  preferred_element_type=F32)


def _experts_call(h2_all, u_b, v_b, r1, e1, cnt, e0):
    n_tok, d = h2_all.shape
    n_exp = u_b.shape[0]
    heads, nk, _ = r1.shape
    tt, eb = 512, 512
    once = pl.Buffered(1)
    aux = lambda: pl.BlockSpec((heads, nk, tt), lambda i, e: (0, 0, i), pipeline_mode=once)
    return pl.pallas_call(
        _experts_kernel,
        grid=(n_tok // tt, n_exp // eb),
        in_specs=[pl.BlockSpec((tt, d), lambda i, e: (i, 0), pipeline_mode=once),
                  pl.BlockSpec((eb, d), lambda i, e: (e, 0)),
                  pl.BlockSpec((eb, d), lambda i, e: (e, 0)),
                  aux(), aux(), aux(), aux()],
        out_specs=pl.BlockSpec((tt, d), lambda i, e: (i, 0)),
        out_shape=jax.ShapeDtypeStruct((n_tok, d), F32),
        scratch_shapes=[pltpu.VMEM((eb, tt), F32), pltpu.VMEM((eb, tt), BF16)],
        compiler_params=_cparams("arbitrary", "arbitrary"),
        name="peer_experts",
    )(h2_all, u_b, v_b, r1, e1, cnt, e0)


def _mixer(x3, mod, h0_re, h0_im, conv_buf, w):
    d = x3.shape[2]
    n_rows = x3.shape[0] * x3.shape[1]
    proj = _inproj_call(x3, mod, w["norm_pre_mix"], w["w_in_b"])
    proj3 = proj.reshape(x3.shape[0], x3.shape[1], -1)
    sw = w["ssm_b_glu"].shape[0]
    cw = w["conv_b_dw"].shape[0]

    t_len, bsz = (x3.shape[0], x3.shape[1]) if mod.time_major else (x3.shape[1], x3.shape[0])
    y_ssm, new_re, new_im = _ssm_branch(proj, bsz, t_len, mod.time_major, h0_re, h0_im, w["ssm"])
    z = _glu_call(y_ssm, w["w_glu_b"], w["ssm_b_glu"])
    y_a = _gated_proj_call(z, w["w_proj_b"], proj, sw + 2 * cw, out_dtype=BF16)

    conv_w = (w["conv_w_dw"], w["conv_b_dw"], w["conv_ln_g"], w["conv_ln_b"], cw, sw)
    if mod.time_major:
        cv, nb_t = _conv_tm_call(proj3, jnp.transpose(conv_buf, (1, 0, 2)), *conv_w, 16)
        new_buf = jnp.transpose(nb_t, (1, 0, 2))
    else:
        if conv_buf is None:
            conv_buf = jnp.zeros((x3.shape[0], CONV_K - 1, cw), F32)
        cv, new_buf = _conv_call(proj3, conv_buf, *conv_w, 1, 256)
    mix = _gated_proj_call(cv.reshape(n_rows, cw), w["w_pw_b"], proj, sw + 2 * cw + d, add=y_a, out_dtype=BF16)
    o = _matmul_call(mix, w["w_out_b"])
    x1, h2 = _post_mix_call(x3, o, mod, w["norm_post_mix"], w["norm_pre_ffn"])
    return x1, h2, new_re, new_im, new_buf


def _layer(xp, xs, c_prompt, c_sample, st_re, st_im, st_conv, w):
    bp, tp, d = xp.shape
    bs, ts, _ = xs.shape

    pad = (-(bp + bs)) % 8
    c_all = jnp.concatenate([c_prompt, c_sample, jnp.zeros((pad, d), F32)], axis=0)
    mod_all = _mod_call(c_all, w["w_mod"], w["b_mod"])
    mod_p = _Mod(mod_all[:bp].reshape(bp, 1, N_MOD * d), False, tp, d)
    mod_s = _Mod(mod_all[bp:bp + bs], True, bs, d)

    x1p, h2p, p_re, p_im, p_buf = _mixer(xp, mod_p, None, None, None, w)
    x1s, h2s, s_re, s_im, s_buf = _mixer(jnp.transpose(xs, (1, 0, 2)), mod_s, st_re, st_im, st_conv, w)

    h2_all = jnp.concatenate([h2p, h2s], axis=0)
    r1, e1, cnt, e0 = _route_call(h2_all, w["w_qt_b"], w["peer_sub_keys"])
    ff = _experts_call(h2_all, w["peer_u_b"], w["peer_v_b"], r1, e1, cnt, e0)

    yp = _final_call(x1p, ff, 0, mod_p, w["norm_post_ffn"])
    ys = jnp.transpose(_final_call(x1s, ff, bp * tp, mod_s, w["norm_post_ffn"]), (1, 0, 2))
    return yp, ys, p_re, p_im, p_buf, s_re, s_im, s_buf


def kernel(x_prompt, x_sample, c_prompt, c_sample, state_ssm_re, state_ssm_im, state_conv, w_mod, b_mod,
           norm_pre_mix, norm_post_mix, norm_pre_ffn, norm_post_ffn, w_in, ssm_lambda_re, ssm_lambda_im,
           ssm_log_dt, ssm_b_re, ssm_b_im, ssm_c_re, ssm_c_im, ssm_d, ssm_w_glu, ssm_b_glu, w_proj_ssm,
           conv_w_dw, conv_b_dw, conv_ln_g, conv_ln_b, conv_w_pw, w_out, peer_w_q, peer_sub_keys, peer_u,
           peer_v):
    depth = w_mod.shape[0]
    xp, xs = x_prompt, x_sample
    outs = [[] for _ in range(6)]
    for l in range(depth):
        w = {
            "w_mod": w_mod[l], "b_mod": b_mod[l],
            "norm_pre_mix": norm_pre_mix[l], "norm_post_mix": norm_post_mix[l],
            "norm_pre_ffn": norm_pre_ffn[l], "norm_post_ffn": norm_post_ffn[l],
            "w_in_b": w_in[l].astype(BF16),
            "ssm": (ssm_lambda_re[l], ssm_lambda_im[l], ssm_log_dt[l], ssm_b_re[l], ssm_b_im[l],
                    ssm_c_re[l], ssm_c_im[l], ssm_d[l]),
            "w_glu_b": ssm_w_glu[l].astype(BF16), "ssm_b_glu": ssm_b_glu[l],
            "w_proj_b": w_proj_ssm[l].astype(BF16),
            "conv_w_dw": conv_w_dw[l], "conv_b_dw": conv_b_dw[l],
            "conv_ln_g": conv_ln_g[l], "conv_ln_b": conv_ln_b[l],
            "w_pw_b": conv_w_pw[l].astype(BF16), "w_out_b": w_out[l].astype(BF16),
            "w_qt_b": jnp.transpose(peer_w_q[l]).astype(BF16),
            "peer_sub_keys": peer_sub_keys[l],
            "peer_u_b": peer_u[l].astype(BF16), "peer_v_b": peer_v[l].astype(BF16),
        }
        xp, xs, *states = _layer(xp, xs, c_prompt, c_sample, state_ssm_re[l], state_ssm_im[l], state_conv[l], w)
        for acc, s in zip(outs, states):
            acc.append(s)
    return (xp, xs) + tuple(jnp.stack(o) for o in outs)
```

```python
import functools
import math

import jax
import jax.numpy as jnp
from jax import lax
from jax.experimental import pallas as pl
from jax.experimental.pallas import tpu as pltpu

F32 = jnp.float32
BF16 = jnp.bfloat16
HIGHEST = lax.Precision.HIGHEST

EPS = 1e-6
N_MOD = 6
SSM_GROUP = 16
SSM_STATE = 64
CONV_K = 31
CONV_HIST = 32
PK_HEADS = 8
PK_NKEYS = 128
PK_TOPK = 16
SSM_CHUNK = 16

V7X_VMEM_BYTES = 64 * 1024 * 1024
VMEM_CAP_MB = 60
LANES = 128
SUBLANES = 8
ROW_TILE = 512
MID_ROW_TILE = 1024
NORM_ROW_TILE = 256
COL_TILE = 1024

PK_CANDS = tuple((k0, k1) for k0 in range(PK_TOPK) for k1 in range(PK_TOPK)
                 if (k0 + 1) * (k1 + 1) <= PK_TOPK)
PK_CAND_ROWS = 56


def _cparams(*sem):
    assert VMEM_CAP_MB * 1024 * 1024 <= V7X_VMEM_BYTES
    return pltpu.CompilerParams(dimension_semantics=sem, vmem_limit_bytes=VMEM_CAP_MB * 1024 * 1024)


def _sigmoid(x):
    return 1.0 / (1.0 + jnp.exp(-x))


def _gelu_tanh(x):
    k0 = 2.0 * math.sqrt(2.0 / math.pi)
    return x / (1.0 + jnp.exp(-(x * (k0 + (k0 * 0.044715) * (x * x)))))


def _rms(x, g):
    return x * lax.rsqrt(jnp.mean(x * x, axis=-1, keepdims=True) + EPS) * g


def _mod_kernel(c_ref, w_ref, b_ref, o_ref):
    c = c_ref[...]
    a = (c * _sigmoid(c)).astype(BF16)
    o_ref[...] = jnp.dot(a, w_ref[...].astype(BF16), preferred_element_type=F32) + b_ref[...]


def _mod_call(c_all, w_mod, b_mod):
    m, d = c_all.shape
    n = w_mod.shape[1]
    tn = 512
    return pl.pallas_call(
        _mod_kernel,
        grid=(n // tn,),
        in_specs=[pl.BlockSpec((m, d), lambda j: (0, 0)),
                  pl.BlockSpec((d, tn), lambda j: (0, j)),
                  pl.BlockSpec((1, tn), lambda j: (0, j))],
        out_specs=pl.BlockSpec((m, tn), lambda j: (0, j)),
        out_shape=jax.ShapeDtypeStruct((m, n), F32),
        compiler_params=_cparams("arbitrary"),
        name="mod",
    )(c_all, w_mod, b_mod.reshape(1, n))


class _Mod:
    def __init__(self, arr, time_major, seq, d):
        self.arr, self.time_major, self.seq, self.d = arr, time_major, seq, d

    def x_spec(self, rows):
        if self.time_major:
            assert rows % self.seq == 0
            return pl.BlockSpec((rows // self.seq, self.seq, self.d), lambda i, *_: (i, 0, 0))
        assert self.seq % rows == 0
        tpb = self.seq // rows
        return pl.BlockSpec((1, rows, self.d), lambda i, *_: (i // tpb, i % tpb, 0))

    def spec(self, k, rows):
        if self.time_major:
            return pl.BlockSpec((self.seq, self.d), lambda i, *_: (0, k))
        tpb = self.seq // rows
        return pl.BlockSpec((None, 1, self.d), lambda i, *_: (i // tpb, 0, k))


NORM_CHUNK_ROWS = 64


def _for_row_chunks(tile_shape, fn):
    n_lead, n_rows, _ = tile_shape
    cr = NORM_CHUNK_ROWS
    per_lead = n_rows // cr

    def body(c, carry):
        sub = pl.ds(pl.multiple_of((c % per_lead) * cr, cr), cr)
        fn(c // per_lead, sub, pl.ds(pl.multiple_of(c * cr, cr), cr))
        return carry

    lax.fori_loop(0, n_lead * per_lead, body, 0)


def _mod_rows(ref, sub):
    return ref[...] if ref.shape[0] == 1 else ref[sub, :]


def _inproj_kernel(x_ref, sh_ref, sc_ref, g_ref, w_ref, o_ref, h_scr):
    @pl.when(pl.program_id(1) == 0)
    def _():
        def chunk(lead, sub, flat):
            h = _rms(x_ref[lead, sub, :], g_ref[...]) * (1.0 + _mod_rows(sc_ref, sub)) + _mod_rows(sh_ref, sub)
            h_scr[flat, :] = h.astype(BF16)

        _for_row_chunks(x_ref.shape, chunk)

    o_ref[...] = jnp.dot(h_scr[...], w_ref[...], preferred_element_type=F32)


def _inproj_call(x3, mod, g, w_b):
    n_rows, d = x3.shape[0] * x3.shape[1], x3.shape[2]
    n = w_b.shape[1]
    tm, tn = ROW_TILE, COL_TILE
    return pl.pallas_call(
        _inproj_kernel,
        grid=(n_rows // tm, n // tn),
        in_specs=[mod.x_spec(tm), mod.spec(0, tm), mod.spec(1, tm),
                  pl.BlockSpec((1, d), lambda i, j: (0, 0)),
                  pl.BlockSpec((d, tn), lambda i, j: (0, j))],
        out_specs=pl.BlockSpec((tm, tn), lambda i, j: (i, j)),
        out_shape=jax.ShapeDtypeStruct((n_rows, n), F32),
        scratch_shapes=[pltpu.VMEM((tm, d), BF16)],
        compiler_params=_cparams("arbitrary", "arbitrary"),
        name="inproj",
    )(x3, mod.arr, mod.arr, g.reshape(1, d), w_b)


def _glu_kernel(y_ref, yt_ref, w_ref, b_ref, o_ref, a_scr):
    @pl.when(pl.program_id(1) == 0)
    def _():
        a_scr[...] = _gelu_tanh(y_ref[...]).astype(BF16)

    acc = jnp.dot(a_scr[...], w_ref[...], preferred_element_type=F32) + b_ref[...]
    o_ref[...] = (_gelu_tanh(yt_ref[...]) * _sigmoid(acc)).astype(BF16)


def _glu_call(y2d, w_b, b):
    n_rows, k = y2d.shape
    n = w_b.shape[1]
    tm, tn = min(MID_ROW_TILE, n_rows), COL_TILE
    return pl.pallas_call(
        _glu_kernel,
        grid=(n_rows // tm, n // tn),
        in_specs=[pl.BlockSpec((tm, k), lambda i, j: (i, 0)),
                  pl.BlockSpec((tm, tn), lambda i, j: (i, j)),
                  pl.BlockSpec((k, tn), lambda i, j: (0, j)),
                  pl.BlockSpec((1, tn), lambda i, j: (0, j))],
        out_specs=pl.BlockSpec((tm, tn), lambda i, j: (i, j)),
        out_shape=jax.ShapeDtypeStruct((n_rows, n), BF16),
        scratch_shapes=[pltpu.VMEM((tm, k), BF16)],
        compiler_params=_cparams("arbitrary", "arbitrary"),
        name="ssm_glu",
    )(y2d, y2d, w_b, b.reshape(1, n))


def _gated_proj_kernel(z_ref, w_ref, gate_ref, o_ref):
    acc = jnp.dot(z_ref[...].astype(BF16), w_ref[...], preferred_element_type=F32)
    o_ref[...] = (_sigmoid(gate_ref[...]) * acc).astype(o_ref.dtype)


def _gated_add_proj_kernel(z_ref, w_ref, gate_ref, a_ref, o_ref):
    acc = jnp.dot(z_ref[...].astype(BF16), w_ref[...], preferred_element_type=F32)
    o_ref[...] = (a_ref[...] + _sigmoid(gate_ref[...]) * acc).astype(o_ref.dtype)


def _gated_proj_call(z2d, w_b, proj, gate_col0, add=None, out_dtype=F32):
    n_rows, k = z2d.shape
    n = w_b.shape[1]
    tm, tn = min(MID_ROW_TILE, n_rows), COL_TILE
    gj = gate_col0 // tn
    in_specs = [pl.BlockSpec((tm, k), lambda i, j: (i, 0)),
                pl.BlockSpec((k, tn), lambda i, j: (0, j)),
                pl.BlockSpec((tm, tn), lambda i, j: (i, gj + j))]
    args = [z2d, w_b, proj]
    if add is not None:
        in_specs.append(pl.BlockSpec((tm, tn), lambda i, j: (i, j)))
        args.append(add)
    return pl.pallas_call(
        _gated_proj_kernel if add is None else _gated_add_proj_kernel,
        grid=(n_rows // tm, n // tn),
        in_specs=in_specs,
        out_specs=pl.BlockSpec((tm, tn), lambda i, j: (i, j)),
        out_shape=jax.ShapeDtypeStruct((n_rows, n), out_dtype),
        compiler_params=_cparams("arbitrary", "arbitrary"),
        name="gated_proj" if add is None else "gated_add_proj",
    )(*args)


def _matmul_kernel(a_ref, w_ref, o_ref):
    o_ref[...] = jnp.dot(a_ref[...], w_ref[...], preferred_element_type=F32)


def _matmul_call(a2d, w_b):
    n_rows, k = a2d.shape
    n = w_b.shape[1]
    tm, tn = min(MID_ROW_TILE, n_rows), COL_TILE
    return pl.pallas_call(
        _matmul_kernel,
        grid=(n_rows // tm, n // tn),
        in_specs=[pl.BlockSpec((tm, k), lambda i, j: (i, 0)),
                  pl.BlockSpec((k, tn), lambda i, j: (0, j))],
        out_specs=pl.BlockSpec((tm, tn), lambda i, j: (i, j)),
        out_shape=jax.ShapeDtypeStruct((n_rows, n), F32),
        compiler_params=_cparams("arbitrary", "arbitrary"),
        name="out_proj",
    )(a2d, w_b)


def _ssm_param_kernel(lrc_ref, lic_ref, lrr_ref, lir_ref, ldt_ref, ctre_ref, ctim_ref, btre_ref, btim_ref,
                      dpad_ref, t_ref, sre_ref, sim_ref, rre_ref, rim_ref, lre_ref, lim_ref, *, chunk):
    p, n = SSM_GROUP, SSM_STATE
    lp = chunk * p
    sre_ref[...] = jnp.zeros_like(sre_ref)
    sim_ref[...] = jnp.zeros_like(sim_ref)
    rre_ref[...] = jnp.zeros_like(rre_ref)
    rim_ref[...] = jnp.zeros_like(rim_ref)
    lane = lax.broadcasted_iota(jnp.int32, (p, lp), 1)
    diag = lax.broadcasted_iota(jnp.int32, (p, lp), 0) == lane
    k_lane = (lax.broadcasted_iota(jnp.int32, (n, lp), 1) // p).astype(F32)
    e_row = (chunk - 1 - lax.broadcasted_iota(jnp.int32, (chunk, n), 0)).astype(F32)
    for a in range(2):
        dt = jnp.exp(ldt_ref[a])
        lrc, lic = lrc_ref[a] * dt, lic_ref[a] * dt
        lrr, lir = lrr_ref[a], lir_ref[a]
        m0 = jnp.exp(lrc * k_lane)
        p0re, p0im = m0 * jnp.cos(lic * k_lane), m0 * jnp.sin(lic * k_lane)
        lbre, lbim = jnp.exp(lrc) * jnp.cos(lic), jnp.exp(lrc) * jnp.sin(lic)
        p1re, p1im = p0re * lbre - p0im * lbim, p0re * lbim + p0im * lbre
        ctre, ctim = ctre_ref[a], ctim_ref[a]
        dec = jnp.exp(lrr * dt)
        abre, abim = dec * jnp.cos(lir * dt), dec * jnp.sin(lir * dt)
        den = lrr * lrr + lir * lir
        fre = ((abre - 1.0) * lrr + abim * lir) / den
        fim = (abim * lrr - (abre - 1.0) * lir) / den
        bre = fre * btre_ref[a] - fim * btim_ref[a]
        bim = fre * btim_ref[a] + fim * btre_ref[a]
        clre = ctre * p0re - ctim * p0im
        clim = ctre * p0im + ctim * p0re
        kt = (jnp.dot(bre, clre, precision=HIGHEST, preferred_element_type=F32)
              - jnp.dot(bim, clim, precision=HIGHEST, preferred_element_type=F32))
        kt = kt + jnp.where(diag, dpad_ref[a], 0.0)
        for s in range(chunk):
            blk = kt if s == 0 else jnp.where(lane >= s * p, pltpu.roll(kt, s * p, 1), 0.0)
            t_ref[a, s * p:(s + 1) * p, :] = blk.astype(BF16)
        mr = jnp.exp(lrr * dt * e_row)
        prre, prim = mr * jnp.cos(lir * dt * e_row), mr * jnp.sin(lir * dt * e_row)
        for s in range(chunk):
            pr, pi = prre[s:s + 1, :], prim[s:s + 1, :]
            r0 = a * lp + s * p
            sre_ref[0, r0:r0 + p, a * n:(a + 1) * n] = bre * pr - bim * pi
            sim_ref[0, r0:r0 + p, a * n:(a + 1) * n] = bre * pi + bim * pr
        rre_ref[0, a * n:(a + 1) * n, a * lp:(a + 1) * lp] = (ctre * p1re - ctim * p1im).astype(BF16)
        rim_ref[0, a * n:(a + 1) * n, a * lp:(a + 1) * lp] = (-(ctre * p1im + ctim * p1re)).astype(BF16)
        ml = jnp.exp(lrr * dt * float(chunk))
        lre_ref[0, :, a * n:(a + 1) * n] = ml * jnp.cos(lir * dt * float(chunk))
        lim_ref[0, :, a * n:(a + 1) * n] = ml * jnp.sin(lir * dt * float(chunk))


def _ssm_param_call(lam_re, lam_im, log_dt, b_re, b_im, c_re, c_im, d_skip, chunk):
    g, n = lam_re.shape
    p = SSM_GROUP
    lp = chunk * p
    ct_re = jnp.tile(jnp.transpose(c_re, (0, 2, 1)), (1, 1, chunk))
    ct_im = jnp.tile(jnp.transpose(c_im, (0, 2, 1)), (1, 1, chunk))
    bt_re = jnp.transpose(b_re, (0, 2, 1))
    bt_im = jnp.transpose(b_im, (0, 2, 1))
    dpad = jnp.pad(d_skip.reshape(g, 1, p), ((0, 0), (0, 0), (0, lp - p)))
    spec3 = lambda s: pl.BlockSpec((2,) + s, lambda i: (i, 0, 0))
    pair = lambda s: pl.BlockSpec((1,) + s, lambda i: (i, 0, 0))
    return pl.pallas_call(
        functools.partial(_ssm_param_kernel, chunk=chunk),
        grid=(g // 2,),
        in_specs=[spec3((n, 1)), spec3((n, 1)), spec3((1, n)), spec3((1, n)), spec3((1, 1)),
                  spec3((n, lp)), spec3((n, lp)), spec3((p, n)), spec3((p, n)), spec3((1, lp))],
        out_specs=[spec3((lp, lp)), pair((2 * lp, 2 * n)), pair((2 * lp, 2 * n)),
                   pair((2 * n, 2 * lp)), pair((2 * n, 2 * lp)), pair((1, 2 * n)), pair((1, 2 * n))],
        out_shape=[jax.ShapeDtypeStruct((g, lp, lp), BF16),
                   jax.ShapeDtypeStruct((g // 2, 2 * lp, 2 * n), F32),
                   jax.ShapeDtypeStruct((g // 2, 2 * lp, 2 * n), F32),
                   jax.ShapeDtypeStruct((g // 2, 2 * n, 2 * lp), BF16),
                   jax.ShapeDtypeStruct((g // 2, 2 * n, 2 * lp), BF16),
                   jax.ShapeDtypeStruct((g // 2, 1, 2 * n), F32),
                   jax.ShapeDtypeStruct((g // 2, 1, 2 * n), F32)],
        compiler_params=_cparams("arbitrary"),
        name="ssm_params",
    )(lam_re.reshape(g, n, 1), lam_im.reshape(g, n, 1), lam_re.reshape(g, 1, n), lam_im.reshape(g, 1, n),
      log_dt.reshape(g, 1, 1), ct_re, ct_im, bt_re, bt_im, dpad)


SSM_LANE_GROUPS = LANES // SSM_GROUP
SSM_ROW_BLK = 16


def _transpose_lane_blocks(load_src, store_dst):
    n = SSM_LANE_GROUPS
    tiles = [load_src(i) for i in range(n)]
    lane_blk = lax.broadcasted_iota(jnp.int32, tiles[0].shape, 1) // SSM_GROUP
    d = 1
    while d < n:
        upper = (lane_blk & d) != 0
        nxt = list(tiles)
        for a in range(n):
            if a & d == 0:
                ta, tb = tiles[a], tiles[a + d]
                nxt[a] = jnp.where(upper, pltpu.roll(tb, d * SSM_GROUP, 1), ta)
                nxt[a + d] = jnp.where(upper, tb, pltpu.roll(ta, LANES - d * SSM_GROUP, 1))
        tiles = nxt
        d *= 2
    for dst in range(n):
        store_dst(dst, tiles[dst])


def _ssm_main_kernel(u_ref, t_ref, sre_ref, sim_ref, rre_ref, rim_ref, lre_ref, lim_ref, h0re_ref, h0im_ref,
                     y_ref, hnre_ref, hnim_ref, x_scr, yg_scr, vre_scr, vim_scr, hre_scr, him_scr,
                     *, batch, chunks, chunk, time_major):
    rows = batch * chunks
    lp = chunk * SSM_GROUP
    rb = SSM_ROW_BLK

    def token_rows(r0, step):
        if time_major:
            return pl.ds(step * batch + r0, rb)
        return pl.ds(r0 * chunk + step, rb, stride=chunk)

    def regroup(i, carry):
        r0 = pl.multiple_of(i * rb, rb)
        for cb in range(lp // LANES):
            lanes = slice(cb * LANES, (cb + 1) * LANES)
            steps = cb * SSM_LANE_GROUPS

            def store_x(g, val):
                x_scr[g, pl.ds(r0, rb), lanes] = val

            _transpose_lane_blocks(lambda s8: u_ref[token_rows(r0, steps + s8), :], store_x)
        return carry

    lax.fori_loop(0, rows // rb, regroup, 0, unroll=2)

    def pair(pr, carry):
        u0, u1 = x_scr[2 * pr], x_scr[2 * pr + 1]
        ucat = jnp.concatenate([u0, u1], axis=1)
        v = jnp.dot(ucat, jnp.concatenate([sre_ref[pr], sim_ref[pr]], axis=1), precision=HIGHEST,
                    preferred_element_type=F32)
        vre_scr[...] = v[:, :LANES]
        vim_scr[...] = v[:, LANES:]
        are, aim = lre_ref[pr], lim_ref[pr]
        if chunks == 1:
            hre, him = h0re_ref[pr], h0im_ref[pr]
            hre_scr[...] = hre
            him_scr[...] = him
            hnre_ref[pr] = are * hre - aim * him + vre_scr[...]
            hnim_ref[pr] = are * him + aim * hre + vim_scr[...]
        else:
            init = (tuple(h0re_ref[pr, b:b + 1, :] for b in range(batch))
                    + tuple(h0im_ref[pr, b:b + 1, :] for b in range(batch)))

            def body(m, hs):
                nre, nim = [], []
                for b in range(batch):
                    hr, hi = hs[b], hs[batch + b]
                    row = b * chunks + m
                    hre_scr[pl.ds(row, 1), :] = hr
                    him_scr[pl.ds(row, 1), :] = hi
                    nre.append(are * hr - aim * hi + vre_scr[pl.ds(row, 1), :])
                    nim.append(are * hi + aim * hr + vim_scr[pl.ds(row, 1), :])
                return tuple(nre) + tuple(nim)

            fin = lax.fori_loop(0, chunks, body, init)
            for b in range(batch):
                hnre_ref[pr, b:b + 1, :] = fin[b]
                hnim_ref[pr, b:b + 1, :] = fin[batch + b]
        hcat = jnp.concatenate([hre_scr[...], him_scr[...]], axis=1).astype(BF16)
        yh = jnp.dot(hcat, jnp.concatenate([rre_ref[pr], rim_ref[pr]], axis=0), preferred_element_type=F32)
        yg_scr[2 * pr] = jnp.dot(u0.astype(BF16), t_ref[2 * pr], preferred_element_type=F32) + yh[:, :lp]
        yg_scr[2 * pr + 1] = jnp.dot(u1.astype(BF16), t_ref[2 * pr + 1], preferred_element_type=F32) + yh[:, lp:]
        return carry

    lax.fori_loop(0, SSM_LANE_GROUPS // 2, pair, 0)

    def ungroup(i, carry):
        r0 = pl.multiple_of(i * rb, rb)
        for cb in range(lp // LANES):
            lanes = slice(cb * LANES, (cb + 1) * LANES)
            steps = cb * SSM_LANE_GROUPS

            def store_y(t8, val):
                y_ref[token_rows(r0, steps + t8), :] = val

            _transpose_lane_blocks(lambda g: yg_scr[g, pl.ds(r0, rb), lanes], store_y)
        return carry

    lax.fori_loop(0, rows // rb, ungroup, 0, unroll=2)


def _ssm_main_call(proj, ops, h0re, h0im, batch, chunks, chunk, time_major):
    t_op, s_re, s_im, r_re, r_im, l_re, l_im = ops
    n_tok = proj.shape[0]
    g, lp, _ = t_op.shape
    rows = batch * chunks
    assert rows % SSM_ROW_BLK == 0
    n2 = 2 * SSM_STATE
    ng = SSM_LANE_GROUPS
    grp = lambda s: pl.BlockSpec((ng,) + s, lambda j: (j, 0, 0))
    pair = lambda s: pl.BlockSpec((ng // 2,) + s, lambda j: (j, 0, 0))
    tok = lambda: pl.BlockSpec((n_tok, LANES), lambda j: (0, j))
    return pl.pallas_call(
        functools.partial(_ssm_main_kernel, batch=batch, chunks=chunks, chunk=chunk, time_major=time_major),
        grid=(g // ng,),
        in_specs=[tok(), grp((lp, lp)), pair((2 * lp, n2)), pair((2 * lp, n2)),
                  pair((n2, 2 * lp)), pair((n2, 2 * lp)), pair((1, n2)), pair((1, n2)),
                  pair((batch, n2)), pair((batch, n2))],
        out_specs=[tok(), pair((batch, n2)), pair((batch, n2))],
        out_shape=[jax.ShapeDtypeStruct((n_tok, g * SSM_GROUP), F32),
                   jax.ShapeDtypeStruct((g // 2, batch, n2), F32),
                   jax.ShapeDtypeStruct((g // 2, batch, n2), F32)],
        scratch_shapes=[pltpu.VMEM((ng, rows, lp), F32)] * 2 + [pltpu.VMEM((rows, n2), F32)] * 4,
        compiler_params=_cparams("arbitrary"),
        name="ssm_main",
    )(proj, t_op, s_re, s_im, r_re, r_im, l_re, l_im, h0re, h0im)


def _ssm_branch(proj, bsz, t_len, time_major, h0_re, h0_im, ssm_w):
    lam_re, lam_im, log_dt, b_re, b_im, c_re, c_im, d_skip = ssm_w
    g, n = lam_re.shape
    chunk = min(SSM_CHUNK, t_len)
    assert t_len % chunk == 0 and (chunk * SSM_GROUP) % LANES == 0
    chunks = t_len // chunk
    assert not time_major or chunks == 1
    ops = _ssm_param_call(lam_re, lam_im, log_dt, b_re, b_im, c_re, c_im, d_skip, chunk)
    if h0_re is None:
        h0re = jnp.zeros((g // 2, bsz, 2 * n), F32)
        h0im = h0re
    else:
        h0re = jnp.transpose(h0_re.reshape(bsz, g // 2, 2 * n), (1, 0, 2))
        h0im = jnp.transpose(h0_im.reshape(bsz, g // 2, 2 * n), (1, 0, 2))
    y, hn_re, hn_im = _ssm_main_call(proj, ops, h0re, h0im, bsz, chunks, chunk, time_major)
    new_re = jnp.transpose(hn_re, (1, 0, 2)).reshape(bsz, g, n)
    new_im = jnp.transpose(hn_im, (1, 0, 2)).reshape(bsz, g, n)
    return y, new_re, new_im


def _conv_kernel(a_ref, g_ref, buf_ref, w_ref, bdw_ref, lng_ref, lnb_ref, cv_ref, nb_ref, up_scr, sh_scr, acc_scr,
                 *, row_blk, col_blk):
    bb, tt, c = a_ref.shape
    hist = CONV_HIST
    off = hist - (CONV_K - 1)
    j = pl.program_id(1)

    @pl.when(j == 0)
    def _():
        up_scr[:, off:hist, :] = buf_ref[...]

    up_scr[:, hist:hist + tt, :] = a_ref[...] * _sigmoid(g_ref[...])
    n_sh = sh_scr.shape[2]
    for r in range(1, SUBLANES):
        sh_scr[r - 1] = up_scr[:, r:r + n_sh, :]
    for r0 in range(0, tt, row_blk):
        for c0 in range(0, c, col_blk):
            acc = jnp.zeros((bb, row_blk, col_blk), F32) + bdw_ref[:, c0:c0 + col_blk][None]
            for k in range(CONV_K):
                r = (k + off) % SUBLANES
                base = r0 + k + off - r
                if r == 0:
                    tap = up_scr[:, base:base + row_blk, c0:c0 + col_blk]
                else:
                    tap = sh_scr[r - 1, :, base:base + row_blk, c0:c0 + col_blk]
                acc = acc + tap * w_ref[k:k + 1, c0:c0 + col_blk][None]
            acc_scr[:, r0:r0 + row_blk, c0:c0 + col_blk] = acc
    v = acc_scr[...]
    mu = jnp.mean(v, axis=-1, keepdims=True)
    vc = v - mu
    var = jnp.mean(vc * vc, axis=-1, keepdims=True)
    y = vc * lax.rsqrt(var + EPS) * lng_ref[...][None] + lnb_ref[...][None]
    cv_ref[...] = (y * _sigmoid(y)).astype(cv_ref.dtype)
    tail = up_scr[:, tt + off:tt + hist, :]

    @pl.when(j == pl.num_programs(1) - 1)
    def _():
        nb_ref[...] = tail

    up_scr[:, off:hist, :] = tail


def _conv_call(proj3, buf, w_dw, b_dw, ln_g, ln_b, c, col0, bb, tt):
    bsz, t_len, _ = proj3.shape
    ja, jg = col0 // c, col0 // c + 1
    row_blk = min(tt, 64)
    vec = lambda: pl.BlockSpec((1, c), lambda i, j: (0, 0))
    return pl.pallas_call(
        functools.partial(_conv_kernel, row_blk=row_blk, col_blk=256),
        grid=(bsz // bb, t_len // tt),
        in_specs=[pl.BlockSpec((bb, tt, c), lambda i, j: (i, j, ja)),
                  pl.BlockSpec((bb, tt, c), lambda i, j: (i, j, jg)),
                  pl.BlockSpec((bb, CONV_K - 1, c), lambda i, j: (i, 0, 0)),
                  pl.BlockSpec((CONV_K, c), lambda i, j: (0, 0)),
                  vec(), vec(), vec()],
        out_specs=[pl.BlockSpec((bb, tt, c), lambda i, j: (i, j, 0)),
                   pl.BlockSpec((bb, CONV_K - 1, c), lambda i, j: (i, 0, 0))],
        out_shape=[jax.ShapeDtypeStruct((bsz, t_len, c), BF16),
                   jax.ShapeDtypeStruct((bsz, CONV_K - 1, c), F32)],
        scratch_shapes=[pltpu.VMEM((bb, CONV_HIST + tt, c), F32),
                        pltpu.VMEM((SUBLANES - 1, bb, CONV_HIST + tt - SUBLANES, c), F32),
                        pltpu.VMEM((bb, tt, c), F32)],
        compiler_params=_cparams("arbitrary", "arbitrary"),
        name="conv",
    )(proj3, proj3, buf, w_dw, b_dw.reshape(1, c), ln_g.reshape(1, c), ln_b.reshape(1, c))


def _conv_tm_kernel(a_ref, g_ref, buf_ref, w_ref, bdw_ref, lng_ref, lnb_ref, cv_ref, nb_ref, up_scr, acc_scr,
                    *, col_blk):
    t_len, bb, c = a_ref.shape
    nh = CONV_K - 1
    up_scr[0:nh] = buf_ref[...]
    up_scr[nh:nh + t_len] = a_ref[...] * _sigmoid(g_ref[...])
    for t in range(t_len):
        for c0 in range(0, c, col_blk):
            acc = jnp.zeros((bb, col_blk), F32) + bdw_ref[:, c0:c0 + col_blk]
            for k in range(CONV_K):
                acc = acc + up_scr[t + k, :, c0:c0 + col_blk] * w_ref[k:k + 1, c0:c0 + col_blk]
            acc_scr[t, :, c0:c0 + col_blk] = acc
    for t in range(t_len):
        v = acc_scr[t]
        mu = jnp.mean(v, axis=-1, keepdims=True)
        vc = v - mu
        var = jnp.mean(vc * vc, axis=-1, keepdims=True)
        y = vc * lax.rsqrt(var + EPS) * lng_ref[...] + lnb_ref[...]
        cv_ref[t] = (y * _sigmoid(y)).astype(cv_ref.dtype)
    nb_ref[...] = up_scr[t_len:t_len + nh]


def _conv_tm_call(proj3, buf_t, w_dw, b_dw, ln_g, ln_b, c, col0, bb):
    t_len, bsz, _ = proj3.shape
    nh = CONV_K - 1
    ja, jg = col0 // c, col0 // c + 1
    vec = lambda: pl.BlockSpec((1, c), lambda i: (0, 0))
    return pl.pallas_call(
        functools.partial(_conv_tm_kernel, col_blk=512),
        grid=(bsz // bb,),
        in_specs=[pl.BlockSpec((t_len, bb, c), lambda i: (0, i, ja)),
                  pl.BlockSpec((t_len, bb, c), lambda i: (0, i, jg)),
                  pl.BlockSpec((nh, bb, c), lambda i: (0, i, 0)),
                  pl.BlockSpec((CONV_K, c), lambda i: (0, 0)),
                  vec(), vec(), vec()],
        out_specs=[pl.BlockSpec((t_len, bb, c), lambda i: (0, i, 0)),
                   pl.BlockSpec((nh, bb, c), lambda i: (0, i, 0))],
        out_shape=[jax.ShapeDtypeStruct((t_len, bsz, c), BF16),
                   jax.ShapeDtypeStruct((nh, bsz, c), F32)],
        scratch_shapes=[pltpu.VMEM((nh + t_len, bb, c), F32), pltpu.VMEM((t_len, bb, c), F32)],
        compiler_params=_cparams("arbitrary"),
        name="conv_tm",
    )(proj3, proj3, buf_t, w_dw, b_dw.reshape(1, c), ln_g.reshape(1, c), ln_b.reshape(1, c))


def _post_mix_kernel(x_ref, o_ref, g1_ref, sh2_ref, sc2_ref, npm_ref, npf_ref, x1_ref, h2_ref):
    def chunk(lead, sub, flat):
        x1 = x_ref[lead, sub, :] + _mod_rows(g1_ref, sub) * _rms(o_ref[flat, :], npm_ref[...])
        x1_ref[lead, sub, :] = x1
        h2 = _rms(x1, npf_ref[...]) * (1.0 + _mod_rows(sc2_ref, sub)) + _mod_rows(sh2_ref, sub)
        h2_ref[flat, :] = h2.astype(BF16)

    _for_row_chunks(x_ref.shape, chunk)


def _post_mix_call(x3, o2d, mod, npm, npf):
    n_rows, d = o2d.shape
    tm = NORM_ROW_TILE
    row = lambda: pl.BlockSpec((tm, d), lambda i: (i, 0))
    vec = lambda: pl.BlockSpec((1, d), lambda i: (0, 0))
    return pl.pallas_call(
        _post_mix_kernel,
        grid=(n_rows // tm,),
        in_specs=[mod.x_spec(tm), row(), mod.spec(2, tm), mod.spec(3, tm), mod.spec(4, tm), vec(), vec()],
        out_specs=[mod.x_spec(tm), row()],
        out_shape=[jax.ShapeDtypeStruct(x3.shape, F32), jax.ShapeDtypeStruct((n_rows, d), BF16)],
        compiler_params=_cparams("arbitrary"),
        name="post_mix",
    )(x3, o2d, mod.arr, mod.arr, mod.arr, npm.reshape(1, d), npf.reshape(1, d))


def _final_kernel(x1_ref, ff_ref, g2_ref, n_ref, o_ref):
    def chunk(lead, sub, flat):
        o_ref[lead, sub, :] = x1_ref[lead, sub, :] + _mod_rows(g2_ref, sub) * _rms(ff_ref[flat, :], n_ref[...])

    _for_row_chunks(x1_ref.shape, chunk)


def _final_call(x1, ff_all, row0, mod, npost):
    n_rows, d = x1.shape[0] * x1.shape[1], x1.shape[2]
    tm = NORM_ROW_TILE
    i0 = row0 // tm
    return pl.pallas_call(
        _final_kernel,
        grid=(n_rows // tm,),
        in_specs=[mod.x_spec(tm),
                  pl.BlockSpec((tm, d), lambda i: (i0 + i, 0)),
                  mod.spec(5, tm),
                  pl.BlockSpec((1, d), lambda i: (0, 0))],
        out_specs=mod.x_spec(tm),
        out_shape=jax.ShapeDtypeStruct(x1.shape, F32),
        compiler_params=_cparams("arbitrary"),
        name="final",
    )(x1, ff_all, mod.arr, npost.reshape(1, d))


def _extract_top(s, n_top):
    rows = s.shape[0]
    rid = lax.broadcasted_iota(jnp.int32, s.shape, 0).astype(F32)
    rank = jnp.full(s.shape, float(rows), F32)
    vals = []
    for it in range(n_top):
        m = jnp.max(s, axis=0, keepdims=True)
        first = jnp.min(jnp.where(s == m, rid, float(rows)), axis=0, keepdims=True)
        sel = rid == first
        rank = jnp.where(sel, float(it), rank)
        s = jnp.where(sel, -jnp.inf, s)
        vals.append(m)
    return vals, rank, jnp.zeros_like(vals[0])


def _extract_top_untied(s, n_top):
    rank = jnp.full(s.shape, float(s.shape[0]), F32)
    vals = []
    for it in range(n_top):
        m = jnp.max(s, axis=0, keepdims=True)
        sel = s == m
        rank = jnp.where(sel, float(it), rank)
        s = jnp.where(sel, -jnp.inf, s)
        vals.append(m)
    taken = jnp.sum(jnp.where(rank < float(n_top), 1.0, 0.0), axis=0, keepdims=True)
    return vals, rank, jnp.abs(taken - float(n_top))


def _route_tables(s0, s1, cand_scr, extract):
    nk = PK_NKEYS
    a_vals, rank0, tied0 = extract(s0, PK_TOPK)
    b_vals, rank1, tied1 = extract(s1, PK_TOPK)
    cand_scr[...] = jnp.full(cand_scr.shape, -jnp.inf, F32)
    for r, (k0, k1) in enumerate(PK_CANDS):
        cand_scr[r:r + 1, :] = a_vals[k0] + b_vals[k1]
    best, rank2, tied2 = extract(cand_scr[...], PK_TOPK)
    z = jnp.zeros_like(best[0])
    for v in best:
        z = z + jnp.exp(v - best[0])
    chosen = jnp.where(rank2 < float(PK_TOPK), 1.0, 0.0)
    cand_row = lax.broadcasted_iota(jnp.int32, (PK_CAND_ROWS, LANES), 0)
    cnt = jnp.zeros((nk, LANES), F32)
    r = 0
    for k0 in range(PK_TOPK):
        n_k0 = sum(1 for cand in PK_CANDS if cand[0] == k0)
        in_k0 = (cand_row >= r) & (cand_row < r + n_k0)
        cnt_k0 = jnp.sum(jnp.where(in_k0, chosen, 0.0), axis=0, keepdims=True)
        cnt = jnp.where(rank0 == float(k0), cnt_k0, cnt)
        r += n_k0
    return (rank1, jnp.exp(s1 - b_vals[0]), cnt, jnp.exp(s0 - a_vals[0]) / z), tied0 + tied1 + tied2


def _route_kernel(h2_ref, wq_ref, keys_ref, r1_ref, e1_ref, cnt_ref, e0_ref, cand_scr):
    nk = PK_NKEYS
    qt = lax.dot_general(wq_ref[...], h2_ref[...], (((1,), (1,)), ((), ())), preferred_element_type=F32)
    s0_all = jnp.dot(keys_ref[0, 0], qt[:nk], precision=HIGHEST, preferred_element_type=F32)
    s1_all = jnp.dot(keys_ref[0, 1], qt[nk:], precision=HIGHEST, preferred_element_type=F32)
    out_refs = (r1_ref, e1_ref, cnt_ref, e0_ref)
    n_chunks = h2_ref.shape[0] // LANES
    tied = []
    for c in range(n_chunks):
        sl = slice(c * LANES, (c + 1) * LANES)
        tables, tied_c = _route_tables(s0_all[:, sl], s1_all[:, sl], cand_scr.at[c], _extract_top_untied)
        tied.append(jnp.max(tied_c))
        for ref, tab in zip(out_refs, tables):
            ref[0, :, sl] = tab.astype(ref.dtype)
    for c in range(n_chunks):
        sl = slice(c * LANES, (c + 1) * LANES)

        @pl.when(tied[c] > 0.0)
        def _():
            exact, _ = _route_tables(s0_all[:, sl], s1_all[:, sl], cand_scr.at[c], _extract_top)
            for ref, tab in zip(out_refs, exact):
                ref[0, :, sl] = tab.astype(ref.dtype)


def _route_call(h2_all, wqt_b, keys):
    n_tok, d = h2_all.shape
    heads, _, nk, half = keys.shape
    tt = 512
    out = lambda: pl.BlockSpec((1, nk, tt), lambda i, h: (h, 0, i))
    tab = lambda dt: jax.ShapeDtypeStruct((heads, nk, n_tok), dt)
    return pl.pallas_call(
        _route_kernel,
        grid=(n_tok // tt, heads),
        in_specs=[pl.BlockSpec((tt, d), lambda i, h: (i, 0)),
                  pl.BlockSpec((2 * half, d), lambda i, h: (h, 0)),
                  pl.BlockSpec((1, 2, nk, half), lambda i, h: (h, 0, 0, 0))],
        out_specs=[out(), out(), out(), out()],
        out_shape=[tab(F32)] * 4,
        scratch_shapes=[pltpu.VMEM((tt // LANES, PK_CAND_ROWS, LANES), F32)],
        compiler_params=_cparams("arbitrary", "arbitrary"),
        name="peer_route",
    )(h2_all, wqt_b, keys)


def _experts_kernel(h2_ref, u_ref, v_ref, r1_ref, e1_ref, cnt_ref, e0_ref, o_ref, act_scr, w_scr):
    eb, tt = act_scr.shape
    nk = PK_NKEYS
    e = pl.program_id(1)

    @pl.when(e == 0)
    def _():
        o_ref[...] = jnp.zeros_like(o_ref)

    st = lax.dot_general(u_ref[...], h2_ref[...], (((1,), (1,)), ((), ())), preferred_element_type=F32)
    act_scr[...] = _gelu_tanh(st)
    for ii in range(eb // nk):
        row = pl.ds(e * (eb // nk) + ii, 1)
        cnt_rows = [cnt_ref[h, row, :] for h in range(PK_HEADS)]
        e0_rows = [e0_ref[h, row, :] for h in range(PK_HEADS)]
        for c in range(tt // LANES):
            sl = slice(c * LANES, (c + 1) * LANES)
            w = jnp.zeros((nk, LANES), F32)
            for h in range(PK_HEADS):
                w = w + jnp.where(r1_ref[h, :, sl] < cnt_rows[h][:, sl], e1_ref[h, :, sl] * e0_rows[h][:, sl], 0.0)
            w_scr[ii * nk:(ii + 1) * nk, sl] = (w * act_scr[ii * nk:(ii + 1) * nk, sl]).astype(BF16)
    o_ref[...] += lax.dot_general(w_scr[...], v_ref[...], (((0,), (0,)), ((), ())),
                                  preferred_element_type=F32)


def _experts_call(h2_all, u_b, v_b, r1, e1, cnt, e0):
    n_tok, d = h2_all.shape
    n_exp = u_b.shape[0]
    heads, nk, _ = r1.shape
    tt, eb = 512, 1024
    once = pl.Buffered(1)
    aux = lambda: pl.BlockSpec((heads, nk, tt), lambda i, e: (0, 0, i), pipeline_mode=once)
    return pl.pallas_call(
        _experts_kernel,
        grid=(n_tok // tt, n_exp // eb),
        in_specs=[pl.BlockSpec((tt, d), lambda i, e: (i, 0), pipeline_mode=once),
                  pl.BlockSpec((eb, d), lambda i, e: (e, 0)),
                  pl.BlockSpec((eb, d), lambda i, e: (e, 0)),
                  aux(), aux(), aux(), aux()],
        out_specs=pl.BlockSpec((tt, d), lambda i, e: (i, 0), pipeline_mode=once),
        out_shape=jax.ShapeDtypeStruct((n_tok, d), F32),
        scratch_shapes=[pltpu.VMEM((eb, tt), F32), pltpu.VMEM((eb, tt), BF16)],
        compiler_params=_cparams("arbitrary", "arbitrary"),
        name="peer_experts",
    )(h2_all, u_b, v_b, r1, e1, cnt, e0)


def _mixer(x3, mod, h0_re, h0_im, conv_buf, w):
    d = x3.shape[2]
    n_rows = x3.shape[0] * x3.shape[1]
    proj = _inproj_call(x3, mod, w["norm_pre_mix"], w["w_in_b"])
    proj3 = proj.reshape(x3.shape[0], x3.shape[1], -1)
    sw = w["ssm_b_glu"].shape[0]
    cw = w["conv_b_dw"].shape[0]

    t_len, bsz = (x3.shape[0], x3.shape[1]) if mod.time_major else (x3.shape[1], x3.shape[0])
    y_ssm, new_re, new_im = _ssm_branch(proj, bsz, t_len, mod.time_major, h0_re, h0_im, w["ssm"])
    z = _glu_call(y_ssm, w["w_glu_b"], w["ssm_b_glu"])
    y_a = _gated_proj_call(z, w["w_proj_b"], proj, sw + 2 * cw, out_dtype=BF16)

    conv_w = (w["conv_w_dw"], w["conv_b_dw"], w["conv_ln_g"], w["conv_ln_b"], cw, sw)
    if mod.time_major:
        cv, nb_t = _conv_tm_call(proj3, jnp.transpose(conv_buf, (1, 0, 2)), *conv_w, 16)
        new_buf = jnp.transpose(nb_t, (1, 0, 2))
    else:
        if conv_buf is None:
            conv_buf = jnp.zeros((x3.shape[0], CONV_K - 1, cw), F32)
        cv, new_buf = _conv_call(proj3, conv_buf, *conv_w, 1, 256)
    mix = _gated_proj_call(cv.reshape(n_rows, cw), w["w_pw_b"], proj, sw + 2 * cw + d, add=y_a, out_dtype=BF16)
    o = _matmul_call(mix, w["w_out_b"])
    x1, h2 = _post_mix_call(x3, o, mod, w["norm_post_mix"], w["norm_pre_ffn"])
    return x1, h2, new_re, new_im, new_buf


def _layer(xp, xs, c_prompt, c_sample, st_re, st_im, st_conv, w):
    bp, tp, d = xp.shape
    bs, ts, _ = xs.shape

    pad = (-(bp + bs)) % 8
    c_all = jnp.concatenate([c_prompt, c_sample, jnp.zeros((pad, d), F32)], axis=0)
    mod_all = _mod_call(c_all, w["w_mod"], w["b_mod"])
    mod_p = _Mod(mod_all[:bp].reshape(bp, 1, N_MOD * d), False, tp, d)
    mod_s = _Mod(mod_all[bp:bp + bs], True, bs, d)

    x1p, h2p, p_re, p_im, p_buf = _mixer(xp, mod_p, None, None, None, w)
    x1s, h2s, s_re, s_im, s_buf = _mixer(jnp.transpose(xs, (1, 0, 2)), mod_s, st_re, st_im, st_conv, w)

    h2_all = jnp.concatenate([h2p, h2s], axis=0)
    r1, e1, cnt, e0 = _route_call(h2_all, w["w_qt_b"], w["peer_sub_keys"])
    ff = _experts_call(h2_all, w["peer_u_b"], w["peer_v_b"], r1, e1, cnt, e0)

    yp = _final_call(x1p, ff, 0, mod_p, w["norm_post_ffn"])
    ys = jnp.transpose(_final_call(x1s, ff, bp * tp, mod_s, w["norm_post_ffn"]), (1, 0, 2))
    return yp, ys, p_re, p_im, p_buf, s_re, s_im, s_buf


def kernel(x_prompt, x_sample, c_prompt, c_sample, state_ssm_re, state_ssm_im, state_conv, w_mod, b_mod,
           norm_pre_mix, norm_post_mix, norm_pre_ffn, norm_post_ffn, w_in, ssm_lambda_re, ssm_lambda_im,
           ssm_log_dt, ssm_b_re, ssm_b_im, ssm_c_re, ssm_c_im, ssm_d, ssm_w_glu, ssm_b_glu, w_proj_ssm,
           conv_w_dw, conv_b_dw, conv_ln_g, conv_ln_b, conv_w_pw, w_out, peer_w_q, peer_sub_keys, peer_u,
           peer_v):
    depth = w_mod.shape[0]
    xp, xs = x_prompt, x_sample
    outs = [[] for _ in range(6)]
    for l in range(depth):
        w = {
            "w_mod": w_mod[l], "b_mod": b_mod[l],
            "norm_pre_mix": norm_pre_mix[l], "norm_post_mix": norm_post_mix[l],
            "norm_pre_ffn": norm_pre_ffn[l], "norm_post_ffn": norm_post_ffn[l],
            "w_in_b": w_in[l].astype(BF16),
            "ssm": (ssm_lambda_re[l], ssm_lambda_im[l], ssm_log_dt[l], ssm_b_re[l], ssm_b_im[l],
                    ssm_c_re[l], ssm_c_im[l], ssm_d[l]),
            "w_glu_b": ssm_w_glu[l].astype(BF16), "ssm_b_glu": ssm_b_glu[l],
            "w_proj_b": w_proj_ssm[l].astype(BF16),
            "conv_w_dw": conv_w_dw[l], "conv_b_dw": conv_b_dw[l],
            "conv_ln_g": conv_ln_g[l], "conv_ln_b": conv_ln_b[l],
            "w_pw_b": conv_w_pw[l].astype(BF16), "w_out_b": w_out[l].astype(BF16),
            "w_qt_b": jnp.transpose(peer_w_q[l]).astype(BF16),
            "peer_sub_keys": peer_sub_keys[l],
            "peer_u_b": peer_u[l].astype(BF16), "peer_v_b": peer_v[l].astype(BF16),
        }
        xp, xs, *states = _layer(xp, xs, c_prompt, c_sample, state_ssm_re[l], state_ssm_im[l], state_conv[l], w)
        for acc, s in zip(outs, states):
            acc.append(s)
    return (xp, xs) + tuple(jnp.stack(o) for o in outs)
```

```python
import functools
import math

import jax
import jax.numpy as jnp
from jax import lax
from jax.experimental import pallas as pl
from jax.experimental.pallas import tpu as pltpu

F32 = jnp.float32
BF16 = jnp.bfloat16
HIGHEST = lax.Precision.HIGHEST

EPS = 1e-6
N_MOD = 6
SSM_GROUP = 16
SSM_STATE = 64
CONV_K = 31
CONV_HIST = 32
PK_HEADS = 8
PK_NKEYS = 128
PK_TOPK = 16
SSM_CHUNK = 16

V7X_VMEM_BYTES = 64 * 1024 * 1024
VMEM_CAP_MB = 60
LANES = 128
SUBLANES = 8
ROW_TILE = 512
MID_ROW_TILE = 1024
NORM_ROW_TILE = 256
COL_TILE = 1024

PK_CANDS = tuple((k0, k1) for k0 in range(PK_TOPK) for k1 in range(PK_TOPK)
                 if (k0 + 1) * (k1 + 1) <= PK_TOPK)
PK_CAND_ROWS = 56


def _cparams(*sem):
    assert VMEM_CAP_MB * 1024 * 1024 <= V7X_VMEM_BYTES
    return pltpu.CompilerParams(dimension_semantics=sem, vmem_limit_bytes=VMEM_CAP_MB * 1024 * 1024)


def _sigmoid(x):
    return 1.0 / (1.0 + jnp.exp(-x))


def _gelu_tanh(x):
    k0 = 2.0 * math.sqrt(2.0 / math.pi)
    return x / (1.0 + jnp.exp(-(x * (k0 + (k0 * 0.044715) * (x * x)))))


def _rms(x, g):
    return x * lax.rsqrt(jnp.mean(x * x, axis=-1, keepdims=True) + EPS) * g


def _mod_kernel(c_ref, w_ref, b_ref, o_ref):
    c = c_ref[...]
    a = (c * _sigmoid(c)).astype(BF16)
    o_ref[...] = jnp.dot(a, w_ref[...].astype(BF16), preferred_element_type=F32) + b_ref[...]


def _mod_call(c_all, w_mod, b_mod):
    m, d = c_all.shape
    n = w_mod.shape[1]
    tn = 512
    return pl.pallas_call(
        _mod_kernel,
        grid=(n // tn,),
        in_specs=[pl.BlockSpec((m, d), lambda j: (0, 0)),
                  pl.BlockSpec((d, tn), lambda j: (0, j)),
                  pl.BlockSpec((1, tn), lambda j: (0, j))],
        out_specs=pl.BlockSpec((m, tn), lambda j: (0, j)),
        out_shape=jax.ShapeDtypeStruct((m, n), F32),
        compiler_params=_cparams("arbitrary"),
        name="mod",
    )(c_all, w_mod, b_mod.reshape(1, n))


class _Mod:
    def __init__(self, arr, time_major, seq, d):
        self.arr, self.time_major, self.seq, self.d = arr, time_major, seq, d

    def x_spec(self, rows):
        if self.time_major:
            assert rows % self.seq == 0
            return pl.BlockSpec((rows // self.seq, self.seq, self.d), lambda i, *_: (i, 0, 0))
        assert self.seq % rows == 0
        tpb = self.seq // rows
        return pl.BlockSpec((1, rows, self.d), lambda i, *_: (i // tpb, i % tpb, 0))

    def spec(self, k, rows):
        if self.time_major:
            return pl.BlockSpec((self.seq, self.d), lambda i, *_: (0, k))
        tpb = self.seq // rows
        return pl.BlockSpec((None, 1, self.d), lambda i, *_: (i // tpb, 0, k))


NORM_CHUNK_ROWS = 64


def _for_row_chunks(tile_shape, fn):
    n_lead, n_rows, _ = tile_shape
    cr = NORM_CHUNK_ROWS
    per_lead = n_rows // cr

    def body(c, carry):
        sub = pl.ds(pl.multiple_of((c % per_lead) * cr, cr), cr)
        fn(c // per_lead, sub, pl.ds(pl.multiple_of(c * cr, cr), cr))
        return carry

    lax.fori_loop(0, n_lead * per_lead, body, 0)


def _mod_rows(ref, sub):
    return ref[...] if ref.shape[0] == 1 else ref[sub, :]


def _inproj_kernel(x_ref, sh_ref, sc_ref, g_ref, w_ref, o_ref, h_scr):
    @pl.when(pl.program_id(1) == 0)
    def _():
        def chunk(lead, sub, flat):
            h = _rms(x_ref[lead, sub, :], g_ref[...]) * (1.0 + _mod_rows(sc_ref, sub)) + _mod_rows(sh_ref, sub)
            h_scr[flat, :] = h.astype(BF16)

        _for_row_chunks(x_ref.shape, chunk)

    o_ref[...] = jnp.dot(h_scr[...], w_ref[...], preferred_element_type=F32)


def _inproj_call(x3, mod, g, w_b):
    n_rows, d = x3.shape[0] * x3.shape[1], x3.shape[2]
    n = w_b.shape[1]
    tm, tn = ROW_TILE, COL_TILE
    return pl.pallas_call(
        _inproj_kernel,
        grid=(n_rows // tm, n // tn),
        in_specs=[mod.x_spec(tm), mod.spec(0, tm), mod.spec(1, tm),
                  pl.BlockSpec((1, d), lambda i, j: (0, 0)),
                  pl.BlockSpec((d, tn), lambda i, j: (0, j))],
        out_specs=pl.BlockSpec((tm, tn), lambda i, j: (i, j)),
        out_shape=jax.ShapeDtypeStruct((n_rows, n), F32),
        scratch_shapes=[pltpu.VMEM((tm, d), BF16)],
        compiler_params=_cparams("arbitrary", "arbitrary"),
        name="inproj",
    )(x3, mod.arr, mod.arr, g.reshape(1, d), w_b)


def _glu_kernel(y_ref, yt_ref, w_ref, b_ref, o_ref, a_scr):
    @pl.when(pl.program_id(1) == 0)
    def _():
        a_scr[...] = _gelu_tanh(y_ref[...]).astype(BF16)

    acc = jnp.dot(a_scr[...], w_ref[...], preferred_element_type=F32) + b_ref[...]
    o_ref[...] = (_gelu_tanh(yt_ref[...]) * _sigmoid(acc)).astype(BF16)


def _glu_call(y2d, w_b, b):
    n_rows, k = y2d.shape
    n = w_b.shape[1]
    tm, tn = min(MID_ROW_TILE, n_rows), COL_TILE
    return pl.pallas_call(
        _glu_kernel,
        grid=(n_rows // tm, n // tn),
        in_specs=[pl.BlockSpec((tm, k), lambda i, j: (i, 0)),
                  pl.BlockSpec((tm, tn), lambda i, j: (i, j)),
                  pl.BlockSpec((k, tn), lambda i, j: (0, j)),
                  pl.BlockSpec((1, tn), lambda i, j: (0, j))],
        out_specs=pl.BlockSpec((tm, tn), lambda i, j: (i, j)),
        out_shape=jax.ShapeDtypeStruct((n_rows, n), BF16),
        scratch_shapes=[pltpu.VMEM((tm, k), BF16)],
        compiler_params=_cparams("arbitrary", "arbitrary"),
        name="ssm_glu",
    )(y2d, y2d, w_b, b.reshape(1, n))


def _gated_proj_kernel(z_ref, w_ref, gate_ref, o_ref):
    acc = jnp.dot(z_ref[...].astype(BF16), w_ref[...], preferred_element_type=F32)
    o_ref[...] = (_sigmoid(gate_ref[...]) * acc).astype(o_ref.dtype)


def _gated_add_proj_kernel(z_ref, w_ref, gate_ref, a_ref, o_ref):
    acc = jnp.dot(z_ref[...].astype(BF16), w_ref[...], preferred_element_type=F32)
    o_ref[...] = (a_ref[...] + _sigmoid(gate_ref[...]) * acc).astype(o_ref.dtype)


def _gated_proj_call(z2d, w_b, proj, gate_col0, add=None, out_dtype=F32):
    n_rows, k = z2d.shape
    n = w_b.shape[1]
    tm, tn = min(MID_ROW_TILE, n_rows), COL_TILE
    gj = gate_col0 // tn
    in_specs = [pl.BlockSpec((tm, k), lambda i, j: (i, 0)),
                pl.BlockSpec((k, tn), lambda i, j: (0, j)),
                pl.BlockSpec((tm, tn), lambda i, j: (i, gj + j))]
    args = [z2d, w_b, proj]
    if add is not None:
        in_specs.append(pl.BlockSpec((tm, tn), lambda i, j: (i, j)))
        args.append(add)
    return pl.pallas_call(
        _gated_proj_kernel if add is None else _gated_add_proj_kernel,
        grid=(n_rows // tm, n // tn),
        in_specs=in_specs,
        out_specs=pl.BlockSpec((tm, tn), lambda i, j: (i, j)),
        out_shape=jax.ShapeDtypeStruct((n_rows, n), out_dtype),
        compiler_params=_cparams("arbitrary", "arbitrary"),
        name="gated_proj" if add is None else "gated_add_proj",
    )(*args)


def _matmul_kernel(a_ref, w_ref, o_ref):
    o_ref[...] = jnp.dot(a_ref[...], w_ref[...], preferred_element_type=F32)


def _matmul_call(a2d, w_b):
    n_rows, k = a2d.shape
    n = w_b.shape[1]
    tm, tn = min(MID_ROW_TILE, n_rows), COL_TILE
    return pl.pallas_call(
        _matmul_kernel,
        grid=(n_rows // tm, n // tn),
        in_specs=[pl.BlockSpec((tm, k), lambda i, j: (i, 0)),
                  pl.BlockSpec((k, tn), lambda i, j: (0, j))],
        out_specs=pl.BlockSpec((tm, tn), lambda i, j: (i, j)),
        out_shape=jax.ShapeDtypeStruct((n_rows, n), F32),
        compiler_params=_cparams("arbitrary", "arbitrary"),
        name="out_proj",
    )(a2d, w_b)


def _ssm_param_kernel(lrc_ref, lic_ref, lrr_ref, lir_ref, ldt_ref, ctre_ref, ctim_ref, btre_ref, btim_ref,
                      dpad_ref, t_ref, sre_ref, sim_ref, rre_ref, rim_ref, lre_ref, lim_ref, *, chunk):
    p, n = SSM_GROUP, SSM_STATE
    lp = chunk * p
    sre_ref[...] = jnp.zeros_like(sre_ref)
    sim_ref[...] = jnp.zeros_like(sim_ref)
    rre_ref[...] = jnp.zeros_like(rre_ref)
    rim_ref[...] = jnp.zeros_like(rim_ref)
    lane = lax.broadcasted_iota(jnp.int32, (p, lp), 1)
    diag = lax.broadcasted_iota(jnp.int32, (p, lp), 0) == lane
    k_lane = (lax.broadcasted_iota(jnp.int32, (n, lp), 1) // p).astype(F32)
    e_row = (chunk - 1 - lax.broadcasted_iota(jnp.int32, (chunk, n), 0)).astype(F32)
    for a in range(2):
        dt = jnp.exp(ldt_ref[a])
        lrc, lic = lrc_ref[a] * dt, lic_ref[a] * dt
        lrr, lir = lrr_ref[a], lir_ref[a]
        m0 = jnp.exp(lrc * k_lane)
        p0re, p0im = m0 * jnp.cos(lic * k_lane), m0 * jnp.sin(lic * k_lane)
        lbre, lbim = jnp.exp(lrc) * jnp.cos(lic), jnp.exp(lrc) * jnp.sin(lic)
        p1re, p1im = p0re * lbre - p0im * lbim, p0re * lbim + p0im * lbre
        ctre, ctim = ctre_ref[a], ctim_ref[a]
        dec = jnp.exp(lrr * dt)
        abre, abim = dec * jnp.cos(lir * dt), dec * jnp.sin(lir * dt)
        den = lrr * lrr + lir * lir
        fre = ((abre - 1.0) * lrr + abim * lir) / den
        fim = (abim * lrr - (abre - 1.0) * lir) / den
        bre = fre * btre_ref[a] - fim * btim_ref[a]
        bim = fre * btim_ref[a] + fim * btre_ref[a]
        clre = ctre * p0re - ctim * p0im
        clim = ctre * p0im + ctim * p0re
        kt = (jnp.dot(bre, clre, precision=HIGHEST, preferred_element_type=F32)
              - jnp.dot(bim, clim, precision=HIGHEST, preferred_element_type=F32))
        kt = kt + jnp.where(diag, dpad_ref[a], 0.0)
        for s in range(chunk):
            blk = kt if s == 0 else jnp.where(lane >= s * p, pltpu.roll(kt, s * p, 1), 0.0)
            t_ref[a, s * p:(s + 1) * p, :] = blk.astype(BF16)
        mr = jnp.exp(lrr * dt * e_row)
        prre, prim = mr * jnp.cos(lir * dt * e_row), mr * jnp.sin(lir * dt * e_row)
        for s in range(chunk):
            pr, pi = prre[s:s + 1, :], prim[s:s + 1, :]
            r0 = a * lp + s * p
            sre_ref[0, r0:r0 + p, a * n:(a + 1) * n] = bre * pr - bim * pi
            sim_ref[0, r0:r0 + p, a * n:(a + 1) * n] = bre * pi + bim * pr
        rre_ref[0, a * n:(a + 1) * n, a * lp:(a + 1) * lp] = (ctre * p1re - ctim * p1im).astype(BF16)
        rim_ref[0, a * n:(a + 1) * n, a * lp:(a + 1) * lp] = (-(ctre * p1im + ctim * p1re)).astype(BF16)
        ml = jnp.exp(lrr * dt * float(chunk))
        lre_ref[0, :, a * n:(a + 1) * n] = ml * jnp.cos(lir * dt * float(chunk))
        lim_ref[0, :, a * n:(a + 1) * n] = ml * jnp.sin(lir * dt * float(chunk))


def _ssm_param_call(lam_re, lam_im, log_dt, b_re, b_im, c_re, c_im, d_skip, chunk):
    g, n = lam_re.shape
    p = SSM_GROUP
    lp = chunk * p
    ct_re = jnp.tile(jnp.transpose(c_re, (0, 2, 1)), (1, 1, chunk))
    ct_im = jnp.tile(jnp.transpose(c_im, (0, 2, 1)), (1, 1, chunk))
    bt_re = jnp.transpose(b_re, (0, 2, 1))
    bt_im = jnp.transpose(b_im, (0, 2, 1))
    dpad = jnp.pad(d_skip.reshape(g, 1, p), ((0, 0), (0, 0), (0, lp - p)))
    spec3 = lambda s: pl.BlockSpec((2,) + s, lambda i: (i, 0, 0))
    pair = lambda s: pl.BlockSpec((1,) + s, lambda i: (i, 0, 0))
    return pl.pallas_call(
        functools.partial(_ssm_param_kernel, chunk=chunk),
        grid=(g // 2,),
        in_specs=[spec3((n, 1)), spec3((n, 1)), spec3((1, n)), spec3((1, n)), spec3((1, 1)),
                  spec3((n, lp)), spec3((n, lp)), spec3((p, n)), spec3((p, n)), spec3((1, lp))],
        out_specs=[spec3((lp, lp)), pair((2 * lp, 2 * n)), pair((2 * lp, 2 * n)),
                   pair((2 * n, 2 * lp)), pair((2 * n, 2 * lp)), pair((1, 2 * n)), pair((1, 2 * n))],
        out_shape=[jax.ShapeDtypeStruct((g, lp, lp), BF16),
                   jax.ShapeDtypeStruct((g // 2, 2 * lp, 2 * n), F32),
                   jax.ShapeDtypeStruct((g // 2, 2 * lp, 2 * n), F32),
                   jax.ShapeDtypeStruct((g // 2, 2 * n, 2 * lp), BF16),
                   jax.ShapeDtypeStruct((g // 2, 2 * n, 2 * lp), BF16),
                   jax.ShapeDtypeStruct((g // 2, 1, 2 * n), F32),
                   jax.ShapeDtypeStruct((g // 2, 1, 2 * n), F32)],
        compiler_params=_cparams("arbitrary"),
        name="ssm_params",
    )(lam_re.reshape(g, n, 1), lam_im.reshape(g, n, 1), lam_re.reshape(g, 1, n), lam_im.reshape(g, 1, n),
      log_dt.reshape(g, 1, 1), ct_re, ct_im, bt_re, bt_im, dpad)


SSM_LANE_GROUPS = LANES // SSM_GROUP
SSM_ROW_BLK = 16


def _transpose_lane_blocks(load_src, store_dst):
    n = SSM_LANE_GROUPS
    tiles = [load_src(i) for i in range(n)]
    lane_blk = lax.broadcasted_iota(jnp.int32, tiles[0].shape, 1) // SSM_GROUP
    d = 1
    while d < n:
        upper = (lane_blk & d) != 0
        nxt = list(tiles)
        for a in range(n):
            if a & d == 0:
                ta, tb = tiles[a], tiles[a + d]
                nxt[a] = jnp.where(upper, pltpu.roll(tb, d * SSM_GROUP, 1), ta)
                nxt[a + d] = jnp.where(upper, tb, pltpu.roll(ta, LANES - d * SSM_GROUP, 1))
        tiles = nxt
        d *= 2
    for dst in range(n):
        store_dst(dst, tiles[dst])


def _ssm_main_kernel(u_ref, t_ref, sre_ref, sim_ref, rre_ref, rim_ref, lre_ref, lim_ref, h0re_ref, h0im_ref,
                     y_ref, hnre_ref, hnim_ref, x_scr, yg_scr, vre_scr, vim_scr, hre_scr, him_scr,
                     *, batch, chunks, chunk, time_major):
    rows = batch * chunks
    lp = chunk * SSM_GROUP
    rb = SSM_ROW_BLK

    def token_rows(r0, step):
        if time_major:
            return pl.ds(step * batch + r0, rb)
        return pl.ds(r0 * chunk + step, rb, stride=chunk)

    def regroup(i, carry):
        r0 = pl.multiple_of(i * rb, rb)
        for cb in range(lp // LANES):
            lanes = slice(cb * LANES, (cb + 1) * LANES)
            steps = cb * SSM_LANE_GROUPS

            def store_x(g, val):
                x_scr[g, pl.ds(r0, rb), lanes] = val

            _transpose_lane_blocks(lambda s8: u_ref[token_rows(r0, steps + s8), :], store_x)
        return carry

    lax.fori_loop(0, rows // rb, regroup, 0, unroll=2)

    def pair(pr, carry):
        u0, u1 = x_scr[2 * pr], x_scr[2 * pr + 1]
        ucat = jnp.concatenate([u0, u1], axis=1)
        v = jnp.dot(ucat, jnp.concatenate([sre_ref[pr], sim_ref[pr]], axis=1), precision=HIGHEST,
                    preferred_element_type=F32)
        vre_scr[...] = v[:, :LANES]
        vim_scr[...] = v[:, LANES:]
        are, aim = lre_ref[pr], lim_ref[pr]
        if chunks == 1:
            hre, him = h0re_ref[pr], h0im_ref[pr]
            hre_scr[...] = hre
            him_scr[...] = him
            hnre_ref[pr] = are * hre - aim * him + vre_scr[...]
            hnim_ref[pr] = are * him + aim * hre + vim_scr[...]
        else:
            init = (tuple(h0re_ref[pr, b:b + 1, :] for b in range(batch))
                    + tuple(h0im_ref[pr, b:b + 1, :] for b in range(batch)))

            def body(m, hs):
                nre, nim = [], []
                for b in range(batch):
                    hr, hi = hs[b], hs[batch + b]
                    row = b * chunks + m
                    hre_scr[pl.ds(row, 1), :] = hr
                    him_scr[pl.ds(row, 1), :] = hi
                    nre.append(are * hr - aim * hi + vre_scr[pl.ds(row, 1), :])
                    nim.append(are * hi + aim * hr + vim_scr[pl.ds(row, 1), :])
                return tuple(nre) + tuple(nim)

            fin = lax.fori_loop(0, chunks, body, init)
            for b in range(batch):
                hnre_ref[pr, b:b + 1, :] = fin[b]
                hnim_ref[pr, b:b + 1, :] = fin[batch + b]
        hcat = jnp.concatenate([hre_scr[...], him_scr[...]], axis=1).astype(BF16)
        yh = jnp.dot(hcat, jnp.concatenate([rre_ref[pr], rim_ref[pr]], axis=0), preferred_element_type=F32)
        yg_scr[2 * pr] = jnp.dot(u0.astype(BF16), t_ref[2 * pr], preferred_element_type=F32) + yh[:, :lp]
        yg_scr[2 * pr + 1] = jnp.dot(u1.astype(BF16), t_ref[2 * pr + 1], preferred_element_type=F32) + yh[:, lp:]
        return carry

    lax.fori_loop(0, SSM_LANE_GROUPS // 2, pair, 0)

    def ungroup(i, carry):
        r0 = pl.multiple_of(i * rb, rb)
        for cb in range(lp // LANES):
            lanes = slice(cb * LANES, (cb + 1) * LANES)
            steps = cb * SSM_LANE_GROUPS

            def store_y(t8, val):
                y_ref[token_rows(r0, steps + t8), :] = val

            _transpose_lane_blocks(lambda g: yg_scr[g, pl.ds(r0, rb), lanes], store_y)
        return carry

    lax.fori_loop(0, rows // rb, ungroup, 0, unroll=2)


def _ssm_main_call(proj, ops, h0re, h0im, batch, chunks, chunk, time_major):
    t_op, s_re, s_im, r_re, r_im, l_re, l_im = ops
    n_tok = proj.shape[0]
    g, lp, _ = t_op.shape
    rows = batch * chunks
    assert rows % SSM_ROW_BLK == 0
    n2 = 2 * SSM_STATE
    ng = SSM_LANE_GROUPS
    grp = lambda s: pl.BlockSpec((ng,) + s, lambda j: (j, 0, 0))
    pair = lambda s: pl.BlockSpec((ng // 2,) + s, lambda j: (j, 0, 0))
    tok = lambda: pl.BlockSpec((n_tok, LANES), lambda j: (0, j))
    return pl.pallas_call(
        functools.partial(_ssm_main_kernel, batch=batch, chunks=chunks, chunk=chunk, time_major=time_major),
        grid=(g // ng,),
        in_specs=[tok(), grp((lp, lp)), pair((2 * lp, n2)), pair((2 * lp, n2)),
                  pair((n2, 2 * lp)), pair((n2, 2 * lp)), pair((1, n2)), pair((1, n2)),
                  pair((batch, n2)), pair((batch, n2))],
        out_specs=[tok(), pair((batch, n2)), pair((batch, n2))],
        out_shape=[jax.ShapeDtypeStruct((n_tok, g * SSM_GROUP), F32),
                   jax.ShapeDtypeStruct((g // 2, batch, n2), F32),
                   jax.ShapeDtypeStruct((g // 2, batch, n2), F32)],
        scratch_shapes=[pltpu.VMEM((ng, rows, lp), F32)] * 2 + [pltpu.VMEM((rows, n2), F32)] * 4,
        compiler_params=_cparams("arbitrary"),
        name="ssm_main",
    )(proj, t_op, s_re, s_im, r_re, r_im, l_re, l_im, h0re, h0im)


def _ssm_branch(proj, bsz, t_len, time_major, h0_re, h0_im, ssm_w):
    lam_re, lam_im, log_dt, b_re, b_im, c_re, c_im, d_skip = ssm_w
    g, n = lam_re.shape
    chunk = min(SSM_CHUNK, t_len)
    assert t_len % chunk == 0 and (chunk * SSM_GROUP) % LANES == 0
    chunks = t_len // chunk
    assert not time_major or chunks == 1
    ops = _ssm_param_call(lam_re, lam_im, log_dt, b_re, b_im, c_re, c_im, d_skip, chunk)
    if h0_re is None:
        h0re = jnp.zeros((g // 2, bsz, 2 * n), F32)
        h0im = h0re
    else:
        h0re = jnp.transpose(h0_re.reshape(bsz, g // 2, 2 * n), (1, 0, 2))
        h0im = jnp.transpose(h0_im.reshape(bsz, g // 2, 2 * n), (1, 0, 2))
    y, hn_re, hn_im = _ssm_main_call(proj, ops, h0re, h0im, bsz, chunks, chunk, time_major)
    new_re = jnp.transpose(hn_re, (1, 0, 2)).reshape(bsz, g, n)
    new_im = jnp.transpose(hn_im, (1, 0, 2)).reshape(bsz, g, n)
    return y, new_re, new_im


def _conv_kernel(a_ref, g_ref, buf_ref, w_ref, bdw_ref, lng_ref, lnb_ref, cv_ref, nb_ref, up_scr, sh_scr, acc_scr,
                 *, row_blk, col_blk):
    bb, tt, c = a_ref.shape
    hist = CONV_HIST
    off = hist - (CONV_K - 1)
    j = pl.program_id(1)

    @pl.when(j == 0)
    def _():
        up_scr[:, off:hist, :] = buf_ref[...]

    up_scr[:, hist:hist + tt, :] = a_ref[...] * _sigmoid(g_ref[...])
    n_sh = sh_scr.shape[2]
    for r in range(1, SUBLANES):
        sh_scr[r - 1] = up_scr[:, r:r + n_sh, :]
    for r0 in range(0, tt, row_blk):
        for c0 in range(0, c, col_blk):
            acc = jnp.zeros((bb, row_blk, col_blk), F32) + bdw_ref[:, c0:c0 + col_blk][None]
            for k in range(CONV_K):
                r = (k + off) % SUBLANES
                base = r0 + k + off - r
                if r == 0:
                    tap = up_scr[:, base:base + row_blk, c0:c0 + col_blk]
                else:
                    tap = sh_scr[r - 1, :, base:base + row_blk, c0:c0 + col_blk]
                acc = acc + tap * w_ref[k:k + 1, c0:c0 + col_blk][None]
            acc_scr[:, r0:r0 + row_blk, c0:c0 + col_blk] = acc
    v = acc_scr[...]
    mu = jnp.mean(v, axis=-1, keepdims=True)
    vc = v - mu
    var = jnp.mean(vc * vc, axis=-1, keepdims=True)
    y = vc * lax.rsqrt(var + EPS) * lng_ref[...][None] + lnb_ref[...][None]
    cv_ref[...] = (y * _sigmoid(y)).astype(cv_ref.dtype)
    tail = up_scr[:, tt + off:tt + hist, :]

    @pl.when(j == pl.num_programs(1) - 1)
    def _():
        nb_ref[...] = tail

    up_scr[:, off:hist, :] = tail


def _conv_call(proj3, buf, w_dw, b_dw, ln_g, ln_b, c, col0, bb, tt):
    bsz, t_len, _ = proj3.shape
    ja, jg = col0 // c, col0 // c + 1
    row_blk = min(tt, 64)
    vec = lambda: pl.BlockSpec((1, c), lambda i, j: (0, 0))
    return pl.pallas_call(
        functools.partial(_conv_kernel, row_blk=row_blk, col_blk=256),
        grid=(bsz // bb, t_len // tt),
        in_specs=[pl.BlockSpec((bb, tt, c), lambda i, j: (i, j, ja)),
                  pl.BlockSpec((bb, tt, c), lambda i, j: (i, j, jg)),
                  pl.BlockSpec((bb, CONV_K - 1, c), lambda i, j: (i, 0, 0)),
                  pl.BlockSpec((CONV_K, c), lambda i, j: (0, 0)),
                  vec(), vec(), vec()],
        out_specs=[pl.BlockSpec((bb, tt, c), lambda i, j: (i, j, 0)),
                   pl.BlockSpec((bb, CONV_K - 1, c), lambda i, j: (i, 0, 0))],
        out_shape=[jax.ShapeDtypeStruct((bsz, t_len, c), BF16),
                   jax.ShapeDtypeStruct((bsz, CONV_K - 1, c), F32)],
        scratch_shapes=[pltpu.VMEM((bb, CONV_HIST + tt, c), F32),
                        pltpu.VMEM((SUBLANES - 1, bb, CONV_HIST + tt - SUBLANES, c), F32),
                        pltpu.VMEM((bb, tt, c), F32)],
        compiler_params=_cparams("arbitrary", "arbitrary"),
        name="conv",
    )(proj3, proj3, buf, w_dw, b_dw.reshape(1, c), ln_g.reshape(1, c), ln_b.reshape(1, c))


def _conv_tm_kernel(a_ref, g_ref, buf_ref, w_ref, bdw_ref, lng_ref, lnb_ref, cv_ref, nb_ref, up_scr, acc_scr,
                    *, col_blk):
    t_len, bb, c = a_ref.shape
    nh = CONV_K - 1
    up_scr[0:nh] = buf_ref[...]
    up_scr[nh:nh + t_len] = a_ref[...] * _sigmoid(g_ref[...])
    for t in range(t_len):
        for c0 in range(0, c, col_blk):
            acc = jnp.zeros((bb, col_blk), F32) + bdw_ref[:, c0:c0 + col_blk]
            for k in range(CONV_K):
                acc = acc + up_scr[t + k, :, c0:c0 + col_blk] * w_ref[k:k + 1, c0:c0 + col_blk]
            acc_scr[t, :, c0:c0 + col_blk] = acc
    for t in range(t_len):
        v = acc_scr[t]
        mu = jnp.mean(v, axis=-1, keepdims=True)
        vc = v - mu
        var = jnp.mean(vc * vc, axis=-1, keepdims=True)
        y = vc * lax.rsqrt(var + EPS) * lng_ref[...] + lnb_ref[...]
        cv_ref[t] = (y * _sigmoid(y)).astype(cv_ref.dtype)
    nb_ref[...] = up_scr[t_len:t_len + nh]


def _conv_tm_call(proj3, buf_t, w_dw, b_dw, ln_g, ln_b, c, col0, bb):
    t_len, bsz, _ = proj3.shape
    nh = CONV_K - 1
    ja, jg = col0 // c, col0 // c + 1
    vec = lambda: pl.BlockSpec((1, c), lambda i: (0, 0))
    return pl.pallas_call(
        functools.partial(_conv_tm_kernel, col_blk=512),
        grid=(bsz // bb,),
        in_specs=[pl.BlockSpec((t_len, bb, c), lambda i: (0, i, ja)),
                  pl.BlockSpec((t_len, bb, c), lambda i: (0, i, jg)),
                  pl.BlockSpec((nh, bb, c), lambda i: (0, i, 0)),
                  pl.BlockSpec((CONV_K, c), lambda i: (0, 0)),
                  vec(), vec(), vec()],
        out_specs=[pl.BlockSpec((t_len, bb, c), lambda i: (0, i, 0)),
                   pl.BlockSpec((nh, bb, c), lambda i: (0, i, 0))],
        out_shape=[jax.ShapeDtypeStruct((t_len, bsz, c), BF16),
                   jax.ShapeDtypeStruct((nh, bsz, c), F32)],
        scratch_shapes=[pltpu.VMEM((nh + t_len, bb, c), F32), pltpu.VMEM((t_len, bb, c), F32)],
        compiler_params=_cparams("arbitrary"),
        name="conv_tm",
    )(proj3, proj3, buf_t, w_dw, b_dw.reshape(1, c), ln_g.reshape(1, c), ln_b.reshape(1, c))


def _post_mix_kernel(x_ref, o_ref, g1_ref, sh2_ref, sc2_ref, npm_ref, npf_ref, x1_ref, h2_ref):
    def chunk(lead, sub, flat):
        x1 = x_ref[lead, sub, :] + _mod_rows(g1_ref, sub) * _rms(o_ref[flat, :], npm_ref[...])
        x1_ref[lead, sub, :] = x1
        h2 = _rms(x1, npf_ref[...]) * (1.0 + _mod_rows(sc2_ref, sub)) + _mod_rows(sh2_ref, sub)
        h2_ref[flat, :] = h2.astype(BF16)

    _for_row_chunks(x_ref.shape, chunk)


def _post_mix_call(x3, o2d, mod, npm, npf):
    n_rows, d = o2d.shape
    tm = NORM_ROW_TILE
    row = lambda: pl.BlockSpec((tm, d), lambda i: (i, 0))
    vec = lambda: pl.BlockSpec((1, d), lambda i: (0, 0))
    return pl.pallas_call(
        _post_mix_kernel,
        grid=(n_rows // tm,),
        in_specs=[mod.x_spec(tm), row(), mod.spec(2, tm), mod.spec(3, tm), mod.spec(4, tm), vec(), vec()],
        out_specs=[mod.x_spec(tm), row()],
        out_shape=[jax.ShapeDtypeStruct(x3.shape, F32), jax.ShapeDtypeStruct((n_rows, d), BF16)],
        compiler_params=_cparams("arbitrary"),
        name="post_mix",
    )(x3, o2d, mod.arr, mod.arr, mod.arr, npm.reshape(1, d), npf.reshape(1, d))


def _final_kernel(x1_ref, ff_ref, g2_ref, n_ref, o_ref):
    def chunk(lead, sub, flat):
        o_ref[lead, sub, :] = x1_ref[lead, sub, :] + _mod_rows(g2_ref, sub) * _rms(ff_ref[flat, :], n_ref[...])

    _for_row_chunks(x1_ref.shape, chunk)


def _final_call(x1, ff_all, row0, mod, npost):
    n_rows, d = x1.shape[0] * x1.shape[1], x1.shape[2]
    tm = NORM_ROW_TILE
    i0 = row0 // tm
    return pl.pallas_call(
        _final_kernel,
        grid=(n_rows // tm,),
        in_specs=[mod.x_spec(tm),
                  pl.BlockSpec((tm, d), lambda i: (i0 + i, 0)),
                  mod.spec(5, tm),
                  pl.BlockSpec((1, d), lambda i: (0, 0))],
        out_specs=mod.x_spec(tm),
        out_shape=jax.ShapeDtypeStruct(x1.shape, F32),
        compiler_params=_cparams("arbitrary"),
        name="final",
    )(x1, ff_all, mod.arr, npost.reshape(1, d))


def _extract_top(s, n_top):
    rows = s.shape[0]
    rid = lax.broadcasted_iota(jnp.int32, s.shape, 0).astype(F32)
    rank = jnp.full(s.shape, float(rows), F32)
    vals = []
    for it in range(n_top):
        m = jnp.max(s, axis=0, keepdims=True)
        first = jnp.min(jnp.where(s == m, rid, float(rows)), axis=0, keepdims=True)
        sel = rid == first
        rank = jnp.where(sel, float(it), rank)
        s = jnp.where(sel, -jnp.inf, s)
        vals.append(m)
    return vals, rank, jnp.zeros_like(vals[0])


def _extract_top_untied(s, n_top):
    rank = jnp.full(s.shape, float(s.shape[0]), F32)
    vals = []
    for it in range(n_top):
        m = jnp.max(s, axis=0, keepdims=True)
        sel = s == m
        rank = jnp.where(sel, float(it), rank)
        s = jnp.where(sel, -jnp.inf, s)
        vals.append(m)
    taken = jnp.sum(jnp.where(rank < float(n_top), 1.0, 0.0), axis=0, keepdims=True)
    return vals, rank, jnp.abs(taken - float(n_top))


def _route_tables(s0, s1, cand_scr, extract):
    nk = PK_NKEYS
    a_vals, rank0, tied0 = extract(s0, PK_TOPK)
    b_vals, rank1, tied1 = extract(s1, PK_TOPK)
    cand_scr[...] = jnp.full(cand_scr.shape, -jnp.inf, F32)
    for r, (k0, k1) in enumerate(PK_CANDS):
        cand_scr[r:r + 1, :] = a_vals[k0] + b_vals[k1]
    best, rank2, tied2 = extract(cand_scr[...], PK_TOPK)
    z = jnp.zeros_like(best[0])
    for v in best:
        z = z + jnp.exp(v - best[0])
    chosen = jnp.where(rank2 < float(PK_TOPK), 1.0, 0.0)
    cand_row = lax.broadcasted_iota(jnp.int32, (PK_CAND_ROWS, LANES), 0)
    cnt = jnp.zeros((nk, LANES), F32)
    r = 0
    for k0 in range(PK_TOPK):
        n_k0 = sum(1 for cand in PK_CANDS if cand[0] == k0)
        in_k0 = (cand_row >= r) & (cand_row < r + n_k0)
        cnt_k0 = jnp.sum(jnp.where(in_k0, chosen, 0.0), axis=0, keepdims=True)
        cnt = jnp.where(rank0 == float(k0), cnt_k0, cnt)
        r += n_k0
    return (rank1, jnp.exp(s1 - b_vals[0]), cnt, jnp.exp(s0 - a_vals[0]) / z), tied0 + tied1 + tied2


def _route_kernel(h2_ref, wq_ref, keys_ref, r1_ref, e1_ref, cnt_ref, e0_ref, cand_scr):
    nk = PK_NKEYS
    qt = lax.dot_general(wq_ref[...], h2_ref[...], (((1,), (1,)), ((), ())), preferred_element_type=F32)
    s0_all = jnp.dot(keys_ref[0, 0], qt[:nk], precision=HIGHEST, preferred_element_type=F32)
    s1_all = jnp.dot(keys_ref[0, 1], qt[nk:], precision=HIGHEST, preferred_element_type=F32)
    out_refs = (r1_ref, e1_ref, cnt_ref, e0_ref)
    n_chunks = h2_ref.shape[0] // LANES
    tied = []
    for c in range(n_chunks):
        sl = slice(c * LANES, (c + 1) * LANES)
        tables, tied_c = _route_tables(s0_all[:, sl], s1_all[:, sl], cand_scr.at[c], _extract_top_untied)
        tied.append(jnp.max(tied_c))
        for ref, tab in zip(out_refs, tables):
            ref[0, :, sl] = tab.astype(ref.dtype)
    for c in range(n_chunks):
        sl = slice(c * LANES, (c + 1) * LANES)

        @pl.when(tied[c] > 0.0)
        def _():
            exact, _ = _route_tables(s0_all[:, sl], s1_all[:, sl], cand_scr.at[c], _extract_top)
            for ref, tab in zip(out_refs, exact):
                ref[0, :, sl] = tab.astype(ref.dtype)


def _route_call(h2_all, wqt_b, keys):
    n_tok, d = h2_all.shape
    heads, _, nk, half = keys.shape
    tt = 1024
    out = lambda: pl.BlockSpec((1, nk, tt), lambda i, h: (h, 0, i))
    tab = lambda dt: jax.ShapeDtypeStruct((heads, nk, n_tok), dt)
    return pl.pallas_call(
        _route_kernel,
        grid=(n_tok // tt, heads),
        in_specs=[pl.BlockSpec((tt, d), lambda i, h: (i, 0)),
                  pl.BlockSpec((2 * half, d), lambda i, h: (h, 0)),
                  pl.BlockSpec((1, 2, nk, half), lambda i, h: (h, 0, 0, 0))],
        out_specs=[out(), out(), out(), out()],
        out_shape=[tab(F32)] * 4,
        scratch_shapes=[pltpu.VMEM((tt // LANES, PK_CAND_ROWS, LANES), F32)],
        compiler_params=_cparams("arbitrary", "arbitrary"),
        name="peer_route",
    )(h2_all, wqt_b, keys)


def _experts_kernel(h2_ref, u_ref, v_ref, r1_ref, e1_ref, cnt_ref, e0_ref, o_ref, act_scr, w_scr):
    eb, tt = act_scr.shape
    nk = PK_NKEYS
    e = pl.program_id(1)

    @pl.when(e == 0)
    def _():
        o_ref[...] = jnp.zeros_like(o_ref)

    st = lax.dot_general(u_ref[...], h2_ref[...], (((1,), (1,)), ((), ())), preferred_element_type=F32)
    act_scr[...] = _gelu_tanh(st)
    for ii in range(eb // nk):
        row = pl.ds(e * (eb // nk) + ii, 1)
        cnt_rows = [cnt_ref[h, row, :] for h in range(PK_HEADS)]
        e0_rows = [e0_ref[h, row, :] for h in range(PK_HEADS)]
        for c in range(tt // LANES):
            sl = slice(c * LANES, (c + 1) * LANES)
            w = jnp.zeros((nk, LANES), F32)
            for h in range(PK_HEADS):
                w = w + jnp.where(r1_ref[h, :, sl] < cnt_rows[h][:, sl], e1_ref[h, :, sl] * e0_rows[h][:, sl], 0.0)
            w_scr[ii * nk:(ii + 1) * nk, sl] = (w * act_scr[ii * nk:(ii + 1) * nk, sl]).astype(BF16)
    o_ref[...] += lax.dot_general(w_scr[...], v_ref[...], (((0,), (0,)), ((), ())),
                                  preferred_element_type=F32)


def _experts_call(h2_all, u_b, v_b, r1, e1, cnt, e0):
    n_tok, d = h2_all.shape
    n_exp = u_b.shape[0]
    heads, nk, _ = r1.shape
    tt, eb = 512, 1024
    once = pl.Buffered(1)
    aux = lambda: pl.BlockSpec((heads, nk, tt), lambda i, e: (0, 0, i), pipeline_mode=once)
    return pl.pallas_call(
        _experts_kernel,
        grid=(n_tok // tt, n_exp // eb),
        in_specs=[pl.BlockSpec((tt, d), lambda i, e: (i, 0), pipeline_mode=once),
                  pl.BlockSpec((eb, d), lambda i, e: (e, 0)),
                  pl.BlockSpec((eb, d), lambda i, e: (e, 0)),
                  aux(), aux(), aux(), aux()],
        out_specs=pl.BlockSpec((tt, d), lambda i, e: (i, 0), pipeline_mode=once),
        out_shape=jax.ShapeDtypeStruct((n_tok, d), F32),
        scratch_shapes=[pltpu.VMEM((eb, tt), F32), pltpu.VMEM((eb, tt), BF16)],
        compiler_params=_cparams("arbitrary", "arbitrary"),
        name="peer_experts",
    )(h2_all, u_b, v_b, r1, e1, cnt, e0)


def _mixer(x3, mod, h0_re, h0_im, conv_buf, w):
    d = x3.shape[2]
    n_rows = x3.shape[0] * x3.shape[1]
    proj = _inproj_call(x3, mod, w["norm_pre_mix"], w["w_in_b"])
    proj3 = proj.reshape(x3.shape[0], x3.shape[1], -1)
    sw = w["ssm_b_glu"].shape[0]
    cw = w["conv_b_dw"].shape[0]

    t_len, bsz = (x3.shape[0], x3.shape[1]) if mod.time_major else (x3.shape[1], x3.shape[0])
    y_ssm, new_re, new_im = _ssm_branch(proj, bsz, t_len, mod.time_major, h0_re, h0_im, w["ssm"])
    z = _glu_call(y_ssm, w["w_glu_b"], w["ssm_b_glu"])
    y_a = _gated_proj_call(z, w["w_proj_b"], proj, sw + 2 * cw, out_dtype=BF16)

    conv_w = (w["conv_w_dw"], w["conv_b_dw"], w["conv_ln_g"], w["conv_ln_b"], cw, sw)
    if mod.time_major:
        cv, nb_t = _conv_tm_call(proj3, jnp.transpose(conv_buf, (1, 0, 2)), *conv_w, 16)
        new_buf = jnp.transpose(nb_t, (1, 0, 2))
    else:
        if conv_buf is None:
            conv_buf = jnp.zeros((x3.shape[0], CONV_K - 1, cw), F32)
        cv, new_buf = _conv_call(proj3, conv_buf, *conv_w, 1, 256)
    mix = _gated_proj_call(cv.reshape(n_rows, cw), w["w_pw_b"], proj, sw + 2 * cw + d, add=y_a, out_dtype=BF16)
    o = _matmul_call(mix, w["w_out_b"])
    x1, h2 = _post_mix_call(x3, o, mod, w["norm_post_mix"], w["norm_pre_ffn"])
    return x1, h2, new_re, new_im, new_buf


def _layer(xp, xs, c_prompt, c_sample, st_re, st_im, st_conv, w):
    bp, tp, d = xp.shape
    bs, ts, _ = xs.shape

    pad = (-(bp + bs)) % 8
    c_all = jnp.concatenate([c_prompt, c_sample, jnp.zeros((pad, d), F32)], axis=0)
    mod_all = _mod_call(c_all, w["w_mod"], w["b_mod"])
    mod_p = _Mod(mod_all[:bp].reshape(bp, 1, N_MOD * d), False, tp, d)
    mod_s = _Mod(mod_all[bp:bp + bs], True, bs, d)

    x1p, h2p, p_re, p_im, p_buf = _mixer(xp, mod_p, None, None, None, w)
    x1s, h2s, s_re, s_im, s_buf = _mixer(jnp.transpose(xs, (1, 0, 2)), mod_s, st_re, st_im, st_conv, w)

    h2_all = jnp.concatenate([h2p, h2s], axis=0)
    r1, e1, cnt, e0 = _route_call(h2_all, w["w_qt_b"], w["peer_sub_keys"])
    ff = _experts_call(h2_all, w["peer_u_b"], w["peer_v_b"], r1, e1, cnt, e0)

    yp = _final_call(x1p, ff, 0, mod_p, w["norm_post_ffn"])
    ys = jnp.transpose(_final_call(x1s, ff, bp * tp, mod_s, w["norm_post_ffn"]), (1, 0, 2))
    return yp, ys, p_re, p_im, p_buf, s_re, s_im, s_buf


def kernel(x_prompt, x_sample, c_prompt, c_sample, state_ssm_re, state_ssm_im, state_conv, w_mod, b_mod,
           norm_pre_mix, norm_post_mix, norm_pre_ffn, norm_post_ffn, w_in, ssm_lambda_re, ssm_lambda_im,
           ssm_log_dt, ssm_b_re, ssm_b_im, ssm_c_re, ssm_c_im, ssm_d, ssm_w_glu, ssm_b_glu, w_proj_ssm,
           conv_w_dw, conv_b_dw, conv_ln_g, conv_ln_b, conv_w_pw, w_out, peer_w_q, peer_sub_keys, peer_u,
           peer_v):
    depth = w_mod.shape[0]
    xp, xs = x_prompt, x_sample
    outs = [[] for _ in range(6)]
    for l in range(depth):
        w = {
            "w_mod": w_mod[l], "b_mod": b_mod[l],
            "norm_pre_mix": norm_pre_mix[l], "norm_post_mix": norm_post_mix[l],
            "norm_pre_ffn": norm_pre_ffn[l], "norm_post_ffn": norm_post_ffn[l],
            "w_in_b": w_in[l].astype(BF16),
            "ssm": (ssm_lambda_re[l], ssm_lambda_im[l], ssm_log_dt[l], ssm_b_re[l], ssm_b_im[l],
                    ssm_c_re[l], ssm_c_im[l], ssm_d[l]),
            "w_glu_b": ssm_w_glu[l].astype(BF16), "ssm_b_glu": ssm_b_glu[l],
            "w_proj_b": w_proj_ssm[l].astype(BF16),
            "conv_w_dw": conv_w_dw[l], "conv_b_dw": conv_b_dw[l],
            "conv_ln_g": conv_ln_g[l], "conv_ln_b": conv_ln_b[l],
            "w_pw_b": conv_w_pw[l].astype(BF16), "w_out_b": w_out[l].astype(BF16),
            "w_qt_b": jnp.transpose(peer_w_q[l]).astype(BF16),
            "peer_sub_keys": peer_sub_keys[l],
            "peer_u_b": peer_u[l].astype(BF16), "peer_v_b": peer_v[l].astype(BF16),
        }
        xp, xs, *states = _layer(xp, xs, c_prompt, c_sample, state_ssm_re[l], state_ssm_im[l], state_conv[l], w)
        for acc, s in zip(outs, states):
            acc.append(s)
    return (xp, xs) + tuple(jnp.stack(o) for o in outs)
```
